```python
import jax
import jax.numpy as jnp
from jax import lax
import numpy as np

D_MODEL = 1024
BATCH = 16
SEQ = 2048
DEPTH = 4

MIX_WIDTH = D_MODEL
POOL_WIDTH = MIX_WIDTH // 2
POOL_WINDOWS = (2, 4, 8, 16)
N_POOL_GROUPS = len(POOL_WINDOWS)
POOL_CH = POOL_WIDTH // N_POOL_GROUPS
MAX_WINDOW = max(POOL_WINDOWS)
SB_WIDTH = MIX_WIDTH - POOL_WIDTH
SB_HEAD_DIM = 64
SB_HEADS = SB_WIDTH // SB_HEAD_DIM
Q_BLOCK = 128
PROJ_WIDTH = POOL_WIDTH + 3 * SB_WIDTH
N_EXPERT_GROUPS = 4
EXPERTS_PER_GROUP = 8
N_EXPERTS = N_EXPERT_GROUPS * EXPERTS_PER_GROUP
TOP_K = 2
D_EXPERT = D_MODEL // 2
DISPATCH_BLOCK = 128
DEEPNORM_ALPHA = (2.0 * DEPTH) ** 0.25
DEEPNORM_BETA = (8.0 * DEPTH) ** -0.25
LN_EPS = 1e-5

kernel_name = "hybrid_pool_stickbreak_hmoe"


def layer_norm(x, g, b):
    xf = x.astype(jnp.float32)
    mu = jnp.mean(xf, axis=-1, keepdims=True)
    xc = xf - mu
    var = jnp.mean(xc * xc, axis=-1, keepdims=True)
    y = xc * lax.rsqrt(var + LN_EPS) * g.astype(jnp.float32) + b.astype(jnp.float32)
    return y.astype(x.dtype)


def multiscale_pool(u, w_pool, pool_scale):
    B, S, _ = u.shape
    ug = u.reshape(B, S, N_POOL_GROUPS, POOL_CH).astype(jnp.float32)
    csum = jnp.cumsum(ug, axis=1)
    cpad = jnp.concatenate(
        [jnp.zeros((B, MAX_WINDOW, N_POOL_GROUPS, POOL_CH), jnp.float32), csum], axis=1)
    pos = jnp.arange(S, dtype=jnp.float32)
    sums, counts = [], []
    for g, w in enumerate(POOL_WINDOWS):
        lo = cpad[:, MAX_WINDOW - w:MAX_WINDOW - w + S, g]
        sums.append(csum[:, :, g] - lo)
        counts.append(jnp.minimum(pos + 1.0, float(w)))
    wsum = jnp.stack(sums, axis=2)
    count = jnp.stack(counts, axis=1)
    pooled = wsum / count[None, :, :, None] - ug
    mixed = jnp.einsum('bsgc,gcd->bsgd', pooled.astype(u.dtype), w_pool)
    return mixed.reshape(B, S, POOL_WIDTH) * pool_scale


def stick_breaking_attention(q, k, v):
    B, S, H, Dh = q.shape
    n_blocks = S // Q_BLOCK
    scale = Dh ** -0.5
    qb = q.reshape(B, n_blocks, Q_BLOCK, H, Dh).transpose(1, 0, 3, 2, 4)
    kh = k.transpose(0, 2, 1, 3)
    vh = v.transpose(0, 2, 1, 3)
    key_pos = jnp.arange(S)

    def block(args):
        qi, bi = args
        z = jnp.einsum('bhqd,bhkd->bhqk', qi, kh).astype(jnp.float32) * scale
        q_pos = bi * Q_BLOCK + jnp.arange(Q_BLOCK)
        mask = key_pos[None, :] < q_pos[:, None]
        log_fail = jnp.where(mask, jax.nn.log_sigmoid(-z), 0.0)
        after = lax.cumsum(log_fail, axis=3, reverse=True) - log_fail
        a = jnp.where(mask, jnp.exp(jax.nn.log_sigmoid(z) + after), 0.0)
        return jnp.einsum('bhqk,bhkd->bhqd', a.astype(vh.dtype), vh)

    o = lax.map(block, (qb, jnp.arange(n_blocks)))
    return o.transpose(1, 0, 3, 2, 4).reshape(B, S, H * Dh)


def hierarchical_moe(h, w_rg, b_rg, w_re, b_re, w_gate, w_up, w_down):
    B, S, D = h.shape
    T = B * S
    xt = h.reshape(T, D)
    tok_idx = jnp.arange(T)
    g_logits = (xt @ w_rg).astype(jnp.float32) + b_rg.astype(jnp.float32)
    g_prob = jax.nn.softmax(g_logits, axis=-1)
    g_idx = jnp.argmax(g_logits, axis=-1).astype(jnp.int32)
    g_p = g_prob[tok_idx, g_idx]
    e_logits = (xt @ w_re).astype(jnp.float32).reshape(T, N_EXPERT_GROUPS, EXPERTS_PER_GROUP)
    e_logits = e_logits + b_re.astype(jnp.float32)
    e_prob = jax.nn.softmax(e_logits[tok_idx, g_idx], axis=-1)
    top_p, top_i = lax.top_k(e_prob, TOP_K)
    top_p = top_p / jnp.sum(top_p, axis=-1, keepdims=True)
    gate = g_p[:, None] * top_p
    expert = g_idx[:, None] * EXPERTS_PER_GROUP + top_i.astype(jnp.int32)

    A = T * TOP_K
    flat_e = expert.reshape(A)
    flat_w = gate.reshape(A)
    flat_tok = jnp.arange(A, dtype=jnp.int32) // TOP_K
    order = jnp.argsort(flat_e)
    se, stok, sw = flat_e[order], flat_tok[order], flat_w[order]
    counts = jax.ops.segment_sum(jnp.ones((A,), jnp.int32), flat_e, num_segments=N_EXPERTS)
    start = jnp.cumsum(counts) - counts
    padded = ((counts + DISPATCH_BLOCK - 1) // DISPATCH_BLOCK) * DISPATCH_BLOCK
    pstart = jnp.cumsum(padded) - padded
    pend = pstart + padded
    dest = pstart[se] + (jnp.arange(A, dtype=jnp.int32) - start[se])
    n_blocks = -(-A // DISPATCH_BLOCK) + N_EXPERTS
    P = n_blocks * DISPATCH_BLOCK
    slot_tok = jnp.zeros((P,), jnp.int32).at[dest].set(stok)
    slot_w = jnp.zeros((P,), jnp.float32).at[dest].set(sw)
    block_start = jnp.arange(n_blocks, dtype=jnp.int32) * DISPATCH_BLOCK
    block_e = jnp.minimum(jnp.sum(block_start[:, None] >= pend[None, :], axis=1),
                          N_EXPERTS - 1).astype(jnp.int32)
    xs = xt[slot_tok].reshape(n_blocks, DISPATCH_BLOCK, D)

    def expert_block(args):
        xb, e = args
        hid = jax.nn.silu(xb @ w_gate[e]) * (xb @ w_up[e])
        return hid @ w_down[e]

    ys = lax.map(expert_block, (xs, block_e)).reshape(P, D)
    out = jnp.zeros((T, D), ys.dtype).at[slot_tok].add(ys * slot_w[:, None].astype(ys.dtype))
    return out.reshape(B, S, D).astype(h.dtype)


def setup_inputs(seed: int = 0) -> dict:
    key = jax.random.key(seed)
    ks = jax.random.split(key, 17)
    nrm = jax.random.normal
    f32 = jnp.float32
    col_scale = jnp.concatenate([
        jnp.full((POOL_WIDTH + 2 * SB_WIDTH,), D_MODEL ** -0.5, f32),
        jnp.full((SB_WIDTH,), D_MODEL ** -0.5 * DEEPNORM_BETA, f32)])
    return {
        "x": nrm(ks[0], (BATCH, SEQ, D_MODEL), f32),
        "w_in": nrm(ks[1], (DEPTH, D_MODEL, PROJ_WIDTH), f32) * col_scale,
        "w_pool": nrm(ks[2], (DEPTH, N_POOL_GROUPS, POOL_CH, POOL_CH), f32) * POOL_CH ** -0.5,
        "pool_scale": 1.0 + 0.1 * nrm(ks[3], (DEPTH, POOL_WIDTH), f32),
        "w_out": nrm(ks[4], (DEPTH, MIX_WIDTH, D_MODEL), f32) * (MIX_WIDTH ** -0.5 * DEEPNORM_BETA),
        "ln1_g": 1.0 + 0.01 * nrm(ks[5], (DEPTH, D_MODEL), f32),
        "ln1_b": 0.01 * nrm(ks[6], (DEPTH, D_MODEL), f32),
        "w_router_group": nrm(ks[7], (DEPTH, D_MODEL, N_EXPERT_GROUPS), f32) * D_MODEL ** -0.5,
        "b_router_group": 0.01 * nrm(ks[8], (DEPTH, N_EXPERT_GROUPS), f32),
        "w_router_expert": nrm(ks[9], (DEPTH, D_MODEL, N_EXPERTS), f32) * D_MODEL ** -0.5,
        "b_router_expert": 0.01 * nrm(ks[10], (DEPTH, N_EXPERT_GROUPS, EXPERTS_PER_GROUP), f32),
        "w_gate": nrm(ks[11], (DEPTH, N_EXPERTS, D_MODEL, D_EXPERT), f32) * D_MODEL ** -0.5,
        "w_up": nrm(ks[12], (DEPTH, N_EXPERTS, D_MODEL, D_EXPERT), f32) * D_MODEL ** -0.5,
        "w_down": nrm(ks[13], (DEPTH, N_EXPERTS, D_EXPERT, D_MODEL), f32) * (D_EXPERT ** -0.5 * DEEPNORM_BETA),
        "ln2_g": 1.0 + 0.01 * nrm(ks[14], (DEPTH, D_MODEL), f32),
        "ln2_b": 0.01 * nrm(ks[15], (DEPTH, D_MODEL), f32),
    }


def reference(x, w_in, w_pool, pool_scale, w_out, ln1_g, ln1_b, w_router_group, b_router_group,
              w_router_expert, b_router_expert, w_gate, w_up, w_down, ln2_g, ln2_b):
    B, S, _ = x.shape
    h = x
    for l in range(DEPTH):
        proj = h @ w_in[l]
        u_pool = proj[..., :POOL_WIDTH]
        q = proj[..., POOL_WIDTH:POOL_WIDTH + SB_WIDTH].reshape(B, S, SB_HEADS, SB_HEAD_DIM)
        k = proj[..., POOL_WIDTH + SB_WIDTH:POOL_WIDTH + 2 * SB_WIDTH].reshape(B, S, SB_HEADS, SB_HEAD_DIM)
        v = proj[..., POOL_WIDTH + 2 * SB_WIDTH:].reshape(B, S, SB_HEADS, SB_HEAD_DIM)
        y_pool = multiscale_pool(u_pool, w_pool[l], pool_scale[l])
        y_sb = stick_breaking_attention(q, k, v)
        mix = jnp.concatenate([y_pool.astype(h.dtype), y_sb.astype(h.dtype)], axis=-1) @ w_out[l]
        h = layer_norm(DEEPNORM_ALPHA * h + mix, ln1_g[l], ln1_b[l])
        moe = hierarchical_moe(h, w_router_group[l], b_router_group[l], w_router_expert[l],
                               b_router_expert[l], w_gate[l], w_up[l], w_down[l])
        h = layer_norm(DEEPNORM_ALPHA * h + moe, ln2_g[l], ln2_b[l])
    return h
```

```python
import functools

import jax
import jax.numpy as jnp
from jax import lax
from jax.experimental import pallas as pl
from jax.experimental.pallas import tpu as pltpu

F32 = jnp.float32
BF16 = jnp.bfloat16

POOL_WINDOWS = (2, 4, 8, 16)
SB_HEAD_DIM = 64
N_EXPERT_GROUPS = 4
EXPERTS_PER_GROUP = 8
N_EXPERTS = N_EXPERT_GROUPS * EXPERTS_PER_GROUP
TOP_K = 2
LN_EPS = 1e-5

LANES = 128
ROW_TILE = 512
ATTN_BLOCK = 256
DISPATCH_ROWS = 256
COMBINE_ROWS = 256
NEG_BIG = -1e30
VMEM_LIMIT = 48 * 1024 * 1024


def _params(sem):
    return pltpu.CompilerParams(dimension_semantics=sem, vmem_limit_bytes=VMEM_LIMIT)


def _proj_kernel(x_ref, w_ref, u_ref, q_ref, k_ref, v_ref):
    xb = x_ref[...].astype(BF16)
    width = u_ref.shape[1]
    for c, o_ref in enumerate((u_ref, q_ref, k_ref, v_ref)):
        r = jnp.dot(xb, w_ref[:, c * width:(c + 1) * width], preferred_element_type=F32)
        o_ref[...] = r.astype(o_ref.dtype)


def _project(h2d, w_in_b):
    T, D = h2d.shape
    width = w_in_b.shape[1] // 4
    blk = lambda: pl.BlockSpec((ROW_TILE, width), lambda i: (i, 0))
    return pl.pallas_call(
        _proj_kernel,
        grid=(T // ROW_TILE,),
        in_specs=[pl.BlockSpec((ROW_TILE, D), lambda i: (i, 0)),
                  pl.BlockSpec(w_in_b.shape, lambda i: (0, 0))],
        out_specs=[blk(), blk(), blk(), blk()],
        out_shape=[jax.ShapeDtypeStruct((T, width), F32)] +
                  [jax.ShapeDtypeStruct((T, width), BF16)] * 3,
        compiler_params=_params(("parallel",)),
        name="proj",
    )(h2d, w_in_b)


def _pool_kernel(u_ref, w_ref, s_ref, o_ref):
    S = u_ref.shape[0]
    row = lax.broadcasted_iota(jnp.int32, (S, LANES), 0)
    for g, win in enumerate(POOL_WINDOWS):
        u = u_ref[:, g * LANES:(g + 1) * LANES]
        acc = u
        k = 1
        while k < win:
            shifted = jnp.where(row >= k, pltpu.roll(acc, k, 0), 0.0)
            acc = acc + shifted
            k *= 2
        count = jnp.minimum(row + 1, win).astype(F32)
        pooled = acc / count - u
        mixed = jnp.dot(pooled.astype(BF16), w_ref[g], preferred_element_type=F32)
        o_ref[:, g * LANES:(g + 1) * LANES] = (
            mixed * s_ref[:, g * LANES:(g + 1) * LANES]).astype(o_ref.dtype)


def _pool(u, w_pool_b, pool_scale, B, S):
    T, W = u.shape
    return pl.pallas_call(
        _pool_kernel,
        grid=(B,),
        in_specs=[pl.BlockSpec((S, W), lambda b: (b, 0)),
                  pl.BlockSpec(w_pool_b.shape, lambda b: (0, 0, 0)),
                  pl.BlockSpec((1, W), lambda b: (0, 0))],
        out_specs=pl.BlockSpec((S, W), lambda b: (b, 0)),
        out_shape=jax.ShapeDtypeStruct((T, W), BF16),
        compiler_params=_params(("parallel",)),
        name="pool",
    )(u, w_pool_b, pool_scale)


def _attn_kernel(q_ref, k_ref, v_ref, o_ref, acc_ref, car_ref):
    i = pl.program_id(2)
    n = ATTN_BLOCK
    q2 = q_ref[...]
    lane = lax.broadcasted_iota(jnp.int32, q2.shape, 1)
    zero = jnp.zeros_like(q2)
    qs = (jnp.where(lane < SB_HEAD_DIM, q2, zero), jnp.where(lane >= SB_HEAD_DIM, q2, zero))
    r_i = lax.broadcasted_iota(jnp.int32, (n, n), 0)
    c_i = lax.broadcasted_iota(jnp.int32, (n, n), 1)
    causal = c_i < r_i
    later = jnp.where(r_i > c_i, 1.0, 0.0).astype(BF16)

    acc_ref[...] = jnp.zeros_like(acc_ref)
    car_ref[...] = jnp.zeros_like(car_ref)

    def block(j, masked):
        start = pl.multiple_of(j * n, n)
        kb = k_ref[pl.ds(start, n), :]
        vb = v_ref[pl.ds(start, n), :]
        for hd in range(2):
            z = lax.dot_general(qs[hd], kb, (((1,), (1,)), ((), ())),
                                preferred_element_type=F32)
            lf = -(jnp.maximum(z, 0.0) + jnp.log(1.0 + jnp.exp(-jnp.abs(z))))
            if masked:
                lf = jnp.where(causal, lf, 0.0)
            hi = lf.astype(BF16)
            lo = (lf - hi.astype(F32)).astype(BF16)
            local = (jnp.dot(hi, later, preferred_element_type=F32) +
                     jnp.dot(lo, later, preferred_element_type=F32))
            a = jnp.exp(z + lf + local + car_ref[hd])
            if masked:
                a = jnp.where(causal, a, 0.0)
            acc_ref[hd] += jnp.dot(a.astype(BF16), vb, preferred_element_type=F32)
            car_ref[hd] += jnp.sum(lf, axis=-1, keepdims=True)

    block(i, True)

    def body(t, carry):
        block(i - 1 - t, False)
        return carry

    lax.fori_loop(0, i, body, 0)
    o_ref[...] = jnp.where(lane < SB_HEAD_DIM, acc_ref[0], acc_ref[1]).astype(o_ref.dtype)


def _attention(q, k, v, B, S):
    T, W = q.shape
    n = ATTN_BLOCK
    nq = S // n
    pairs = W // LANES
    return pl.pallas_call(
        _attn_kernel,
        grid=(B, pairs, nq),
        in_specs=[pl.BlockSpec((n, LANES), lambda b, p, i: (b * nq + i, p)),
                  pl.BlockSpec((S, LANES), lambda b, p, i: (b, p)),
                  pl.BlockSpec((S, LANES), lambda b, p, i: (b, p))],
        out_specs=pl.BlockSpec((n, LANES), lambda b, p, i: (b * nq + i, p)),
        out_shape=jax.ShapeDtypeStruct((T, W), BF16),
        scratch_shapes=[pltpu.VMEM((2, n, LANES), F32), pltpu.VMEM((2, n, 1), F32)],
        compiler_params=_params(("parallel", "parallel", "parallel")),
        name="attn",
    )(q, k, v)


def _layer_norm(y, g, b):
    mu = jnp.mean(y, axis=-1, keepdims=True)
    yc = y - mu
    var = jnp.mean(yc * yc, axis=-1, keepdims=True)
    return yc * lax.rsqrt(var + LN_EPS) * g + b


def _mix_kernel(alpha, yp_ref, ys_ref, h_ref, wo_ref, g_ref, b_ref, wrh_ref, wrl_ref, br_ref,
                h1_ref, route_ref):
    half = yp_ref.shape[1]
    mix = (jnp.dot(yp_ref[...], wo_ref[:half, :], preferred_element_type=F32) +
           jnp.dot(ys_ref[...], wo_ref[half:, :], preferred_element_type=F32))
    h1 = _layer_norm(alpha * h_ref[...] + mix, g_ref[...], b_ref[...])
    h1_ref[...] = h1

    hh = h1.astype(BF16)
    hl = (h1 - hh.astype(F32)).astype(BF16)
    logits = (jnp.dot(hh, wrh_ref[...], preferred_element_type=F32) +
              jnp.dot(hl, wrh_ref[...], preferred_element_type=F32) +
              jnp.dot(hh, wrl_ref[...], preferred_element_type=F32) + br_ref[...])
    col = lax.broadcasted_iota(jnp.int32, logits.shape, 1)
    gl = jnp.where(col < N_EXPERT_GROUPS, logits, NEG_BIG)
    gmax = jnp.max(gl, axis=-1, keepdims=True)
    gidx = jnp.min(jnp.where(gl == gmax, col, LANES), axis=-1, keepdims=True)
    g_p = 1.0 / jnp.sum(jnp.exp(gl - gmax), axis=-1, keepdims=True)
    lo_col = N_EXPERT_GROUPS + EXPERTS_PER_GROUP * gidx
    el = jnp.where(col >= lo_col, jnp.where(col < lo_col + EXPERTS_PER_GROUP, logits, NEG_BIG),
                   NEG_BIG)
    m1 = jnp.max(el, axis=-1, keepdims=True)
    i1 = jnp.min(jnp.where(el == m1, col, LANES), axis=-1, keepdims=True)
    el2 = jnp.where(col == i1, NEG_BIG, el)
    m2 = jnp.max(el2, axis=-1, keepdims=True)
    i2 = jnp.min(jnp.where(el2 == m2, col, LANES), axis=-1, keepdims=True)
    ratio = jnp.exp(m2 - m1)
    gate1 = g_p / (1.0 + ratio)
    gate2 = g_p * ratio / (1.0 + ratio)
    e1 = (i1 - N_EXPERT_GROUPS).astype(F32)
    e2 = (i2 - N_EXPERT_GROUPS).astype(F32)
    route_ref[...] = jnp.where(col == 0, e1, jnp.where(col == 1, e2,
                               jnp.where(col == 2, gate1, jnp.where(col == 3, gate2, 0.0))))


def _mix_norm_route(alpha, y_pool, y_sb, h2d, w_out_b, g, b, wr_hi, wr_lo, b_r):
    T, D = h2d.shape
    half = y_pool.shape[1]
    full = lambda a: pl.BlockSpec(a.shape, lambda i: (0,) * a.ndim)
    return pl.pallas_call(
        functools.partial(_mix_kernel, alpha),
        grid=(T // ROW_TILE,),
        in_specs=[pl.BlockSpec((ROW_TILE, half), lambda i: (i, 0)),
                  pl.BlockSpec((ROW_TILE, half), lambda i: (i, 0)),
                  pl.BlockSpec((ROW_TILE, D), lambda i: (i, 0)),
                  full(w_out_b), full(g), full(b), full(wr_hi), full(wr_lo), full(b_r)],
        out_specs=[pl.BlockSpec((ROW_TILE, D), lambda i: (i, 0)),
                   pl.BlockSpec((ROW_TILE, LANES), lambda i: (i, 0))],
        out_shape=[jax.ShapeDtypeStruct((T, D), F32), jax.ShapeDtypeStruct((T, LANES), F32)],
        compiler_params=_params(("parallel",)),
        name="mix_norm_route",
    )(y_pool, y_sb, h2d, w_out_b, g, b, wr_hi, wr_lo, b_r)


def _row_gather_copy(src_hbm, row, dst_ref, dst_row, sem):
    return pltpu.make_async_copy(src_hbm.at[pl.ds(row, 1)], dst_ref.at[pl.ds(dst_row, 1)], sem)


def _expert_kernel(be_ref, nv_ref, tok_ref, h_hbm, wg_ref, wu_ref, wd_ref, ys_ref,
                   xbuf, sem, wgb, wub, wdb):
    i = pl.program_id(0)
    nb = pl.num_programs(0)
    rows = DISPATCH_ROWS

    def issue(blk, slot):
        @pl.when(nv_ref[blk] > 0)
        def _():
            def body(r, carry):
                _row_gather_copy(h_hbm, tok_ref[blk * rows + r], xbuf.at[slot], r,
                                 sem.at[slot]).start()
                return carry
            lax.fori_loop(0, rows, body, 0)

    @pl.when(i == 0)
    def _():
        issue(0, 0)

    @pl.when(i + 1 < nb)
    def _():
        issue(i + 1, (i + 1) % 2)

    slot = i % 2
    e = be_ref[i]
    prev = be_ref[jnp.maximum(i - 1, 0)]

    @pl.when(jnp.logical_or(i == 0, e != prev))
    def _():
        wgb[...] = wg_ref[...].astype(BF16)
        wub[...] = wu_ref[...].astype(BF16)
        wdb[...] = wd_ref[...].astype(BF16)

    @pl.when(nv_ref[i] > 0)
    def _():
        pltpu.make_async_copy(h_hbm.at[pl.ds(0, rows)], xbuf.at[slot], sem.at[slot]).wait()
        x = xbuf[slot].astype(BF16)
        gate = jnp.dot(x, wgb[...], preferred_element_type=F32)
        up = jnp.dot(x, wub[...], preferred_element_type=F32)
        hid = (gate * jax.nn.sigmoid(gate) * up).astype(BF16)
        ys_ref[...] = jnp.dot(hid, wdb[...], preferred_element_type=F32)

    @pl.when(nv_ref[i] <= 0)
    def _():
        ys_ref[...] = jnp.zeros_like(ys_ref)


def _experts(layer, block_e, nvalid, slot_tok, h1, w_gate, w_up, w_down):
    T, D = h1.shape
    DE = w_gate.shape[-1]
    n_blocks = block_e.shape[0]
    rows = DISPATCH_ROWS
    grid_spec = pltpu.PrefetchScalarGridSpec(
        num_scalar_prefetch=3,
        grid=(n_blocks,),
        in_specs=[pl.BlockSpec(memory_space=pl.ANY),
                  pl.BlockSpec((None, None, D, DE), lambda i, be, nv, tok: (layer, be[i], 0, 0)),
                  pl.BlockSpec((None, None, D, DE), lambda i, be, nv, tok: (layer, be[i], 0, 0)),
                  pl.BlockSpec((None, None, DE, D), lambda i, be, nv, tok: (layer, be[i], 0, 0))],
        out_specs=pl.BlockSpec((rows, D), lambda i, be, nv, tok: (i, 0)),
        scratch_shapes=[pltpu.VMEM((2, rows, D), F32), pltpu.SemaphoreType.DMA((2,)),
                        pltpu.VMEM((D, DE), BF16), pltpu.VMEM((D, DE), BF16),
                        pltpu.VMEM((DE, D), BF16)],
    )
    return pl.pallas_call(
        _expert_kernel,
        grid_spec=grid_spec,
        out_shape=jax.ShapeDtypeStruct((n_blocks * rows, D), F32),
        compiler_params=_params(("arbitrary",)),
        name="experts",
    )(block_e, nvalid, slot_tok, h1, w_gate, w_up, w_down)


def _combine_kernel(alpha, pos_ref, ys_hbm, h_ref, route_ref, g_ref, b_ref, o_ref, ybuf, sem):
    i = pl.program_id(0)
    nb = pl.num_programs(0)
    rows = COMBINE_ROWS

    def issue(blk, slot):
        def body(r, carry):
            for k in range(TOP_K):
                _row_gather_copy(ys_hbm, pos_ref[(blk * rows + r) * TOP_K + k],
                                 ybuf.at[slot, k], r, sem.at[slot]).start()
            return carry
        lax.fori_loop(0, rows, body, 0)

    @pl.when(i == 0)
    def _():
        issue(0, 0)

    @pl.when(i + 1 < nb)
    def _():
        issue(i + 1, (i + 1) % 2)

    slot = i % 2
    for k in range(TOP_K):
        pltpu.make_async_copy(ys_hbm.at[pl.ds(0, rows)], ybuf.at[slot, k], sem.at[slot]).wait()
    route = route_ref[...]
    moe = route[:, 2:3] * ybuf[slot, 0] + route[:, 3:4] * ybuf[slot, 1]
    o_ref[...] = _layer_norm(alpha * h_ref[...] + moe, g_ref[...], b_ref[...])


def _combine_norm(alpha, pos, ys, h1, route, g, b):
    T, D = h1.shape
    rows = COMBINE_ROWS
    grid_spec = pltpu.PrefetchScalarGridSpec(
        num_scalar_prefetch=1,
        grid=(T // rows,),
        in_specs=[pl.BlockSpec(memory_space=pl.ANY),
                  pl.BlockSpec((rows, D), lambda i, pos: (i, 0)),
                  pl.BlockSpec((rows, LANES), lambda i, pos: (i, 0)),
                  pl.BlockSpec((1, D), lambda i, pos: (0, 0)),
                  pl.BlockSpec((1, D), lambda i, pos: (0, 0))],
        out_specs=pl.BlockSpec((rows, D), lambda i, pos: (i, 0)),
        scratch_shapes=[pltpu.VMEM((2, TOP_K, rows, D), F32), pltpu.SemaphoreType.DMA((2,))],
    )
    return pl.pallas_call(
        functools.partial(_combine_kernel, alpha),
        grid_spec=grid_spec,
        out_shape=jax.ShapeDtypeStruct((T, D), F32),
        compiler_params=_params(("arbitrary",)),
        name="combine_norm",
    )(pos, ys, h1, route, g, b)


def _dispatch_plan(route, n_blocks):
    T = route.shape[0]
    A = T * TOP_K
    rows = DISPATCH_ROWS
    flat_e = route[:, :TOP_K].astype(jnp.int32).reshape(A)
    order = jnp.argsort(flat_e, stable=True).astype(jnp.int32)
    se = flat_e[order]
    counts = jnp.sum((flat_e[:, None] == jnp.arange(N_EXPERTS, dtype=jnp.int32)[None, :])
                     .astype(jnp.int32), axis=0)
    start = jnp.cumsum(counts) - counts
    padded = ((counts + rows - 1) // rows) * rows
    pstart = jnp.cumsum(padded) - padded
    pend = pstart + padded
    block_start = jnp.arange(n_blocks, dtype=jnp.int32) * rows
    block_e = jnp.minimum(jnp.sum((block_start[:, None] >= pend[None, :]).astype(jnp.int32), axis=1),
                          N_EXPERTS - 1).astype(jnp.int32)
    nvalid = jnp.clip(counts[block_e] - (block_start - pstart[block_e]), 0, rows).astype(jnp.int32)
    p = jnp.arange(n_blocks * rows, dtype=jnp.int32)
    e_p = block_e[p // rows]
    r = p - pstart[e_p]
    valid = jnp.logical_and(r >= 0, r < counts[e_p])
    src = jnp.where(valid, start[e_p] + r, 0)
    slot_tok = jnp.where(valid, order[src] // TOP_K, 0).astype(jnp.int32)
    dest = pstart[se] + (jnp.arange(A, dtype=jnp.int32) - start[se])
    pos = jnp.zeros((A,), jnp.int32).at[order].set(dest.astype(jnp.int32))
    return block_e, nvalid, slot_tok, pos


def kernel(x, w_in, w_pool, pool_scale, w_out, ln1_g, ln1_b, w_router_group, b_router_group,
           w_router_expert, b_router_expert, w_gate, w_up, w_down, ln2_g, ln2_b):
    B, S, D = x.shape
    depth = w_in.shape[0]
    T = B * S
    alpha = (2.0 * depth) ** 0.25
    pool_width = w_pool.shape[1] * w_pool.shape[2]
    sb_width = (w_in.shape[2] - pool_width) // 3
    n_blocks = -(-T * TOP_K // DISPATCH_ROWS) + N_EXPERTS

    col_scale = jnp.concatenate([jnp.ones((pool_width,), F32),
                                 jnp.full((sb_width,), SB_HEAD_DIM ** -0.5, F32),
                                 jnp.ones((2 * sb_width,), F32)])
    w_in_b = (w_in * col_scale).astype(BF16)
    w_pool_b = w_pool.astype(BF16)
    w_out_b = w_out.astype(BF16)
    w_r = jnp.concatenate([w_router_group, w_router_expert,
                           jnp.zeros((depth, D, LANES - N_EXPERT_GROUPS - N_EXPERTS), F32)], axis=-1)
    wr_hi = w_r.astype(BF16)
    wr_lo = (w_r - wr_hi.astype(F32)).astype(BF16)
    b_r = jnp.concatenate([b_router_group.astype(F32),
                           b_router_expert.astype(F32).reshape(depth, N_EXPERTS),
                           jnp.zeros((depth, LANES - N_EXPERT_GROUPS - N_EXPERTS), F32)], axis=-1)

    h = x.reshape(T, D)
    for l in range(depth):
        u, q, k, v = _project(h, w_in_b[l])
        y_pool = _pool(u, w_pool_b[l], pool_scale[l][None, :], B, S)
        y_sb = _attention(q, k, v, B, S)
        h1, route = _mix_norm_route(alpha, y_pool, y_sb, h, w_out_b[l], ln1_g[l][None, :],
                                    ln1_b[l][None, :], wr_hi[l], wr_lo[l], b_r[l][None, :])
        block_e, nvalid, slot_tok, pos = _dispatch_plan(route, n_blocks)
        ys = _experts(l, block_e, nvalid, slot_tok, h1, w_gate, w_up, w_down)
        h = _combine_norm(alpha, pos, ys, h1, route, ln2_g[l][None, :], ln2_b[l][None, :])
    return h.reshape(B, S, D)
```

```python
import functools

import jax
import jax.numpy as jnp
from jax import lax
from jax.experimental import pallas as pl
from jax.experimental.pallas import tpu as pltpu

F32 = jnp.float32
BF16 = jnp.bfloat16

POOL_WINDOWS = (2, 4, 8, 16)
SB_HEAD_DIM = 64
N_EXPERT_GROUPS = 4
EXPERTS_PER_GROUP = 8
N_EXPERTS = N_EXPERT_GROUPS * EXPERTS_PER_GROUP
TOP_K = 2
LN_EPS = 1e-5

LANES = 128
ROW_TILE = 512
ATTN_BLOCK = 256
DISPATCH_ROWS = 256
DISPATCH_TILE = 256
COMBINE_ROWS = 256
NEG_BIG = -1e30
VMEM_LIMIT = 48 * 1024 * 1024


def _params(sem):
    return pltpu.CompilerParams(dimension_semantics=sem, vmem_limit_bytes=VMEM_LIMIT)


def _proj_kernel(x_ref, w_ref, u_ref, q_ref, k_ref, v_ref):
    xb = x_ref[...].astype(BF16)
    width = u_ref.shape[1]
    for c, o_ref in enumerate((u_ref, q_ref, k_ref, v_ref)):
        r = jnp.dot(xb, w_ref[:, c * width:(c + 1) * width], preferred_element_type=F32)
        o_ref[...] = r.astype(o_ref.dtype)


def _project(h2d, w_in_b):
    T, D = h2d.shape
    width = w_in_b.shape[1] // 4
    blk = lambda: pl.BlockSpec((ROW_TILE, width), lambda i: (i, 0))
    return pl.pallas_call(
        _proj_kernel,
        grid=(T // ROW_TILE,),
        in_specs=[pl.BlockSpec((ROW_TILE, D), lambda i: (i, 0)),
                  pl.BlockSpec(w_in_b.shape, lambda i: (0, 0))],
        out_specs=[blk(), blk(), blk(), blk()],
        out_shape=[jax.ShapeDtypeStruct((T, width), F32)] +
                  [jax.ShapeDtypeStruct((T, width), BF16)] * 3,
        compiler_params=_params(("parallel",)),
        name="proj",
    )(h2d, w_in_b)


def _pool_kernel(u_ref, w_ref, s_ref, o_ref):
    S = u_ref.shape[0]
    row = lax.broadcasted_iota(jnp.int32, (S, LANES), 0)
    for g, win in enumerate(POOL_WINDOWS):
        u = u_ref[:, g * LANES:(g + 1) * LANES]
        acc = u
        k = 1
        while k < win:
            shifted = jnp.where(row >= k, pltpu.roll(acc, k, 0), 0.0)
            acc = acc + shifted
            k *= 2
        count = jnp.minimum(row + 1, win).astype(F32)
        pooled = acc / count - u
        mixed = jnp.dot(pooled.astype(BF16), w_ref[g], preferred_element_type=F32)
        o_ref[:, g * LANES:(g + 1) * LANES] = (
            mixed * s_ref[:, g * LANES:(g + 1) * LANES]).astype(o_ref.dtype)


def _pool(u, w_pool_b, pool_scale, B, S):
    T, W = u.shape
    return pl.pallas_call(
        _pool_kernel,
        grid=(B,),
        in_specs=[pl.BlockSpec((S, W), lambda b: (b, 0)),
                  pl.BlockSpec(w_pool_b.shape, lambda b: (0, 0, 0)),
                  pl.BlockSpec((1, W), lambda b: (0, 0))],
        out_specs=pl.BlockSpec((S, W), lambda b: (b, 0)),
        out_shape=jax.ShapeDtypeStruct((T, W), BF16),
        compiler_params=_params(("parallel",)),
        name="pool",
    )(u, w_pool_b, pool_scale)


def _attn_kernel(q_ref, k_ref, v_ref, o_ref, acc_ref, car_ref):
    i = pl.program_id(2)
    n = ATTN_BLOCK
    q2 = q_ref[...]
    lane = lax.broadcasted_iota(jnp.int32, q2.shape, 1)
    zero = jnp.zeros_like(q2)
    qs = (jnp.where(lane < SB_HEAD_DIM, q2, zero), jnp.where(lane >= SB_HEAD_DIM, q2, zero))
    r_i = lax.broadcasted_iota(jnp.int32, (n, n), 0)
    c_i = lax.broadcasted_iota(jnp.int32, (n, n), 1)
    causal = c_i < r_i
    later = jnp.where(r_i > c_i, 1.0, 0.0).astype(BF16)

    acc_ref[...] = jnp.zeros_like(acc_ref)
    car_ref[...] = jnp.zeros_like(car_ref)

    def block(j, masked):
        start = pl.multiple_of(j * n, n)
        kb = k_ref[pl.ds(start, n), :]
        vb = v_ref[pl.ds(start, n), :]
        for hd in range(2):
            z = lax.dot_general(qs[hd], kb, (((1,), (1,)), ((), ())),
                                preferred_element_type=F32)
            lf = -(jnp.maximum(z, 0.0) + jnp.log(1.0 + jnp.exp(-jnp.abs(z))))
            if masked:
                lf = jnp.where(causal, lf, 0.0)
            hi = lf.astype(BF16)
            lo = (lf - hi.astype(F32)).astype(BF16)
            local = (jnp.dot(hi, later, preferred_element_type=F32) +
                     jnp.dot(lo, later, preferred_element_type=F32))
            a = jnp.exp(z + lf + local + car_ref[hd])
            if masked:
                a = jnp.where(causal, a, 0.0)
            acc_ref[hd] += jnp.dot(a.astype(BF16), vb, preferred_element_type=F32)
            car_ref[hd] += jnp.sum(lf, axis=-1, keepdims=True)

    block(i, True)

    def body(t, carry):
        block(i - 1 - t, False)
        return carry

    lax.fori_loop(0, i, body, 0)
    o_ref[...] = jnp.where(lane < SB_HEAD_DIM, acc_ref[0], acc_ref[1]).astype(o_ref.dtype)


def _attention(q, k, v, B, S):
    T, W = q.shape
    n = ATTN_BLOCK
    nq = S // n
    pairs = W // LANES
    return pl.pallas_call(
        _attn_kernel,
        grid=(B, pairs, nq),
        in_specs=[pl.BlockSpec((n, LANES), lambda b, p, i: (b * nq + i, p)),
                  pl.BlockSpec((S, LANES), lambda b, p, i: (b, p)),
                  pl.BlockSpec((S, LANES), lambda b, p, i: (b, p))],
        out_specs=pl.BlockSpec((n, LANES), lambda b, p, i: (b * nq + i, p)),
        out_shape=jax.ShapeDtypeStruct((T, W), BF16),
        scratch_shapes=[pltpu.VMEM((2, n, LANES), F32), pltpu.VMEM((2, n, 1), F32)],
        compiler_params=_params(("parallel", "parallel", "parallel")),
        name="attn",
    )(q, k, v)


def _layer_norm(y, g, b):
    mu = jnp.mean(y, axis=-1, keepdims=True)
    yc = y - mu
    var = jnp.mean(yc * yc, axis=-1, keepdims=True)
    return yc * lax.rsqrt(var + LN_EPS) * g + b


def _mix_kernel(alpha, yp_ref, ys_ref, h_ref, wo_ref, g_ref, b_ref, wrh_ref, wrl_ref, br_ref,
                h1_ref, route_ref, cnt_ref, run_ref):
    @pl.when(pl.program_id(0) == 0)
    def _():
        run_ref[...] = jnp.zeros_like(run_ref)

    half = yp_ref.shape[1]
    mix = (jnp.dot(yp_ref[...], wo_ref[:half, :], preferred_element_type=F32) +
           jnp.dot(ys_ref[...], wo_ref[half:, :], preferred_element_type=F32))
    h1 = _layer_norm(alpha * h_ref[...] + mix, g_ref[...], b_ref[...])
    h1_ref[...] = h1

    hh = h1.astype(BF16)
    hl = (h1 - hh.astype(F32)).astype(BF16)
    logits = (jnp.dot(hh, wrh_ref[...], preferred_element_type=F32) +
              jnp.dot(hl, wrh_ref[...], preferred_element_type=F32) +
              jnp.dot(hh, wrl_ref[...], preferred_element_type=F32) + br_ref[...])
    col = lax.broadcasted_iota(jnp.int32, logits.shape, 1)
    gl = jnp.where(col < N_EXPERT_GROUPS, logits, NEG_BIG)
    gmax = jnp.max(gl, axis=-1, keepdims=True)
    gidx = jnp.min(jnp.where(gl == gmax, col, LANES), axis=-1, keepdims=True)
    g_p = 1.0 / jnp.sum(jnp.exp(gl - gmax), axis=-1, keepdims=True)
    lo_col = N_EXPERT_GROUPS + EXPERTS_PER_GROUP * gidx
    el = jnp.where(col >= lo_col, jnp.where(col < lo_col + EXPERTS_PER_GROUP, logits, NEG_BIG),
                   NEG_BIG)
    m1 = jnp.max(el, axis=-1, keepdims=True)
    i1 = jnp.min(jnp.where(el == m1, col, LANES), axis=-1, keepdims=True)
    el2 = jnp.where(col == i1, NEG_BIG, el)
    m2 = jnp.max(el2, axis=-1, keepdims=True)
    i2 = jnp.min(jnp.where(el2 == m2, col, LANES), axis=-1, keepdims=True)
    ratio = jnp.exp(m2 - m1)
    gate1 = g_p / (1.0 + ratio)
    gate2 = g_p * ratio / (1.0 + ratio)
    e1 = (i1 - N_EXPERT_GROUPS).astype(F32)
    e2 = (i2 - N_EXPERT_GROUPS).astype(F32)

    rows = logits.shape[0]
    hit1 = col == i1
    hit2 = col == i2
    onehot = jnp.where(hit1, 1.0, jnp.where(hit2, 1.0, 0.0))
    r_i = lax.broadcasted_iota(jnp.int32, (rows, rows), 0)
    c_i = lax.broadcasted_iota(jnp.int32, (rows, rows), 1)
    earlier = jnp.where(c_i < r_i, 1.0, 0.0).astype(BF16)
    before = run_ref[...] + jnp.dot(earlier, onehot.astype(BF16), preferred_element_type=F32)
    rank1 = jnp.sum(jnp.where(hit1, before, 0.0), axis=-1, keepdims=True)
    rank2 = jnp.sum(jnp.where(hit2, before, 0.0), axis=-1, keepdims=True)
    run_ref[...] += jnp.sum(onehot, axis=0, keepdims=True)
    cnt_ref[...] = run_ref[...]

    route_ref[...] = jnp.where(col == 0, e1, jnp.where(col == 1, e2,
                               jnp.where(col == 2, gate1, jnp.where(col == 3, gate2,
                               jnp.where(col == 4, rank1, jnp.where(col == 5, rank2, 0.0))))))


def _mix_norm_route(alpha, y_pool, y_sb, h2d, w_out_b, g, b, wr_hi, wr_lo, b_r):
    T, D = h2d.shape
    half = y_pool.shape[1]
    full = lambda a: pl.BlockSpec(a.shape, lambda i: (0,) * a.ndim)
    return pl.pallas_call(
        functools.partial(_mix_kernel, alpha),
        grid=(T // ROW_TILE,),
        in_specs=[pl.BlockSpec((ROW_TILE, half), lambda i: (i, 0)),
                  pl.BlockSpec((ROW_TILE, half), lambda i: (i, 0)),
                  pl.BlockSpec((ROW_TILE, D), lambda i: (i, 0)),
                  full(w_out_b), full(g), full(b), full(wr_hi), full(wr_lo), full(b_r)],
        out_specs=[pl.BlockSpec((ROW_TILE, D), lambda i: (i, 0)),
                   pl.BlockSpec((ROW_TILE, LANES), lambda i: (i, 0)),
                   pl.BlockSpec((1, LANES), lambda i: (0, 0))],
        out_shape=[jax.ShapeDtypeStruct((T, D), F32), jax.ShapeDtypeStruct((T, LANES), F32),
                   jax.ShapeDtypeStruct((1, LANES), F32)],
        scratch_shapes=[pltpu.VMEM((1, LANES), F32)],
        compiler_params=_params(("arbitrary",)),
        name="mix_norm_route",
    )(y_pool, y_sb, h2d, w_out_b, g, b, wr_hi, wr_lo, b_r)


def _row_gather_copy(src_hbm, row, dst_ref, dst_row, sem):
    return pltpu.make_async_copy(src_hbm.at[pl.ds(row, 1)], dst_ref.at[pl.ds(dst_row, 1)], sem)


def _row_scatter_copy(src_ref, src_row, dst_hbm, row, sem):
    return pltpu.make_async_copy(src_ref.at[pl.ds(src_row, 1)], dst_hbm.at[pl.ds(row, 1)], sem)


def _dispatch_kernel(pos_ref, tail_ref, nvb_ref, h_ref, xs_hbm, sbuf, zbuf, sem, zsem):
    i = pl.program_id(0)
    nb = pl.num_programs(0)
    rows = DISPATCH_TILE
    blk = DISPATCH_ROWS
    n_blocks = xs_hbm.shape[0] // blk

    def zero_copy(start):
        return pltpu.make_async_copy(zbuf, xs_hbm.at[pl.ds(pl.multiple_of(start, blk), blk)], zsem)

    @pl.when(i == 0)
    def _():
        zbuf[...] = jnp.zeros_like(zbuf)
        for e in range(N_EXPERTS):
            @pl.when(tail_ref[e] >= 0)
            def _():
                zero_copy(tail_ref[e]).start()

        def start_unused(b, carry):
            zero_copy(b * blk).start()
            return carry

        def wait_unused(b, carry):
            zero_copy(b * blk).wait()
            return carry

        lax.fori_loop(nvb_ref[0], n_blocks, start_unused, 0)
        for e in range(N_EXPERTS):
            @pl.when(tail_ref[e] >= 0)
            def _():
                zero_copy(tail_ref[e]).wait()
        lax.fori_loop(nvb_ref[0], n_blocks, wait_unused, 0)

    slot = i % 2

    def wait_slot(s):
        for _ in range(TOP_K):
            pltpu.make_async_copy(sbuf.at[s], xs_hbm.at[pl.ds(0, rows)], sem.at[s]).wait()

    @pl.when(i >= 2)
    def _():
        wait_slot(slot)

    sbuf[slot] = h_ref[...]

    def body(r, carry):
        for k in range(TOP_K):
            _row_scatter_copy(sbuf.at[slot], r, xs_hbm, pos_ref[(i * rows + r) * TOP_K + k],
                              sem.at[slot]).start()
        return carry

    lax.fori_loop(0, rows, body, 0, unroll=8)

    @pl.when(i == nb - 1)
    def _():
        @pl.when(nb >= 2)
        def _():
            wait_slot(1 - slot)
        wait_slot(slot)


def _dispatch(pos, tail, n_valid_blocks, h1, n_blocks):
    T, D = h1.shape
    rows = DISPATCH_TILE
    grid_spec = pltpu.PrefetchScalarGridSpec(
        num_scalar_prefetch=3,
        grid=(T // rows,),
        in_specs=[pl.BlockSpec((rows, D), lambda i, pos, tail, nvb: (i, 0))],
        out_specs=pl.BlockSpec(memory_space=pl.ANY),
        scratch_shapes=[pltpu.VMEM((2, rows, D), F32), pltpu.VMEM((DISPATCH_ROWS, D), F32),
                        pltpu.SemaphoreType.DMA((2,)), pltpu.SemaphoreType.DMA],
    )
    return pl.pallas_call(
        _dispatch_kernel,
        grid_spec=grid_spec,
        out_shape=jax.ShapeDtypeStruct((n_blocks * DISPATCH_ROWS, D), F32),
        compiler_params=_params(("arbitrary",)),
        name="dispatch",
    )(pos, tail, n_valid_blocks, h1)


def _expert_kernel(be_ref, nvb_ref, xs_ref, wg_ref, wu_ref, wd_ref, ys_ref, wgb, wub, wdb):
    i = pl.program_id(0)
    e = be_ref[i]
    prev = be_ref[jnp.maximum(i - 1, 0)]

    @pl.when(jnp.logical_or(i == 0, e != prev))
    def _():
        wgb[...] = wg_ref[...].astype(BF16)
        wub[...] = wu_ref[...].astype(BF16)
        wdb[...] = wd_ref[...].astype(BF16)

    @pl.when(i < nvb_ref[0])
    def _():
        x = xs_ref[...].astype(BF16)
        gate = jnp.dot(x, wgb[...], preferred_element_type=F32)
        up = jnp.dot(x, wub[...], preferred_element_type=F32)
        hid = (gate * jax.nn.sigmoid(gate) * up).astype(BF16)
        ys_ref[...] = jnp.dot(hid, wdb[...], preferred_element_type=F32)

    @pl.when(i >= nvb_ref[0])
    def _():
        ys_ref[...] = jnp.zeros_like(ys_ref)


def _experts(layer, block_e, n_valid_blocks, xs, w_gate, w_up, w_down):
    P, D = xs.shape
    DE = w_gate.shape[-1]
    rows = DISPATCH_ROWS
    n_blocks = P // rows
    x_map = lambda i, be, nvb: (jnp.minimum(i, jnp.maximum(nvb[0] - 1, 0)), 0)
    grid_spec = pltpu.PrefetchScalarGridSpec(
        num_scalar_prefetch=2,
        grid=(n_blocks,),
        in_specs=[pl.BlockSpec((rows, D), x_map),
                  pl.BlockSpec((None, None, D, DE), lambda i, be, nvb: (layer, be[i], 0, 0)),
                  pl.BlockSpec((None, None, D, DE), lambda i, be, nvb: (layer, be[i], 0, 0)),
                  pl.BlockSpec((None, None, DE, D), lambda i, be, nvb: (layer, be[i], 0, 0))],
        out_specs=pl.BlockSpec((rows, D), lambda i, be, nvb: (i, 0)),
        scratch_shapes=[pltpu.VMEM((D, DE), BF16), pltpu.VMEM((D, DE), BF16),
                        pltpu.VMEM((DE, D), BF16)],
    )
    return pl.pallas_call(
        _expert_kernel,
        grid_spec=grid_spec,
        out_shape=jax.ShapeDtypeStruct((P, D), F32),
        compiler_params=_params(("arbitrary",)),
        name="experts",
    )(block_e, n_valid_blocks, xs, w_gate, w_up, w_down)


def _combine_kernel(alpha, pos_ref, ys_hbm, h_ref, route_ref, g_ref, b_ref, o_ref, ybuf, sem):
    i = pl.program_id(0)
    nb = pl.num_programs(0)
    rows = COMBINE_ROWS

    def issue(blk, slot):
        def body(r, carry):
            for k in range(TOP_K):
                _row_gather_copy(ys_hbm, pos_ref[(blk * rows + r) * TOP_K + k],
                                 ybuf.at[slot, k], r, sem.at[slot]).start()
            return carry
        lax.fori_loop(0, rows, body, 0, unroll=8)

    @pl.when(i == 0)
    def _():
        issue(0, 0)

    @pl.when(i + 1 < nb)
    def _():
        issue(i + 1, (i + 1) % 2)

    slot = i % 2
    for k in range(TOP_K):
        pltpu.make_async_copy(ys_hbm.at[pl.ds(0, rows)], ybuf.at[slot, k], sem.at[slot]).wait()
    route = route_ref[...]
    moe = route[:, 2:3] * ybuf[slot, 0] + route[:, 3:4] * ybuf[slot, 1]
    o_ref[...] = _layer_norm(alpha * h_ref[...] + moe, g_ref[...], b_ref[...])


def _combine_norm(alpha, pos, ys, h1, route, g, b):
    T, D = h1.shape
    rows = COMBINE_ROWS
    grid_spec = pltpu.PrefetchScalarGridSpec(
        num_scalar_prefetch=1,
        grid=(T // rows,),
        in_specs=[pl.BlockSpec(memory_space=pl.ANY),
                  pl.BlockSpec((rows, D), lambda i, pos: (i, 0)),
                  pl.BlockSpec((rows, LANES), lambda i, pos: (i, 0)),
                  pl.BlockSpec((1, D), lambda i, pos: (0, 0)),
                  pl.BlockSpec((1, D), lambda i, pos: (0, 0))],
        out_specs=pl.BlockSpec((rows, D), lambda i, pos: (i, 0)),
        scratch_shapes=[pltpu.VMEM((2, TOP_K, rows, D), F32), pltpu.SemaphoreType.DMA((2,))],
    )
    return pl.pallas_call(
        functools.partial(_combine_kernel, alpha),
        grid_spec=grid_spec,
        out_shape=jax.ShapeDtypeStruct((T, D), F32),
        compiler_params=_params(("arbitrary",)),
        name="combine_norm",
    )(pos, ys, h1, route, g, b)


def _dispatch_plan(route, counts_row, n_blocks):
    rows = DISPATCH_ROWS
    experts = jnp.arange(N_EXPERTS, dtype=jnp.int32)
    counts = counts_row[0, N_EXPERT_GROUPS:N_EXPERT_GROUPS + N_EXPERTS].astype(jnp.int32)
    padded = ((counts + rows - 1) // rows) * rows
    pend = jnp.cumsum(padded)
    pstart = pend - padded
    block_start = jnp.arange(n_blocks, dtype=jnp.int32) * rows
    block_e = jnp.minimum(jnp.sum((block_start[:, None] >= pend[None, :]).astype(jnp.int32), axis=1),
                          N_EXPERTS - 1).astype(jnp.int32)
    n_valid_blocks = (pend[-1:] // rows).astype(jnp.int32)
    tail = jnp.where(counts > 0, pend - rows, -1).astype(jnp.int32)
    e = route[:, :TOP_K].astype(jnp.int32)
    rank = route[:, 4:4 + TOP_K].astype(jnp.int32)
    base = jnp.sum(jnp.where(e[:, :, None] == experts[None, None, :], pstart[None, None, :], 0),
                   axis=-1)
    pos = (base + rank).reshape(-1).astype(jnp.int32)
    return block_e, n_valid_blocks, tail, pos


def kernel(x, w_in, w_pool, pool_scale, w_out, ln1_g, ln1_b, w_router_group, b_router_group,
           w_router_expert, b_router_expert, w_gate, w_up, w_down, ln2_g, ln2_b):
    B, S, D = x.shape
    depth = w_in.shape[0]
    T = B * S
    alpha = (2.0 * depth) ** 0.25
    pool_width = w_pool.shape[1] * w_pool.shape[2]
    sb_width = (w_in.shape[2] - pool_width) // 3
    n_blocks = -(-T * TOP_K // DISPATCH_ROWS) + N_EXPERTS

    col_scale = jnp.concatenate([jnp.ones((pool_width,), F32),
                                 jnp.full((sb_width,), SB_HEAD_DIM ** -0.5, F32),
                                 jnp.ones((2 * sb_width,), F32)])
    w_in_b = (w_in * col_scale).astype(BF16)
    w_pool_b = w_pool.astype(BF16)
    w_out_b = w_out.astype(BF16)
    w_r = jnp.concatenate([w_router_group, w_router_expert,
                           jnp.zeros((depth, D, LANES - N_EXPERT_GROUPS - N_EXPERTS), F32)], axis=-1)
    wr_hi = w_r.astype(BF16)
    wr_lo = (w_r - wr_hi.astype(F32)).astype(BF16)
    b_r = jnp.concatenate([b_router_group.astype(F32),
                           b_router_expert.astype(F32).reshape(depth, N_EXPERTS),
                           jnp.zeros((depth, LANES - N_EXPERT_GROUPS - N_EXPERTS), F32)], axis=-1)

    h = x.reshape(T, D)
    for l in range(depth):
        u, q, k, v = _project(h, w_in_b[l])
        y_pool = _pool(u, w_pool_b[l], pool_scale[l][None, :], B, S)
        y_sb = _attention(q, k, v, B, S)
        h1, route, counts = _mix_norm_route(alpha, y_pool, y_sb, h, w_out_b[l], ln1_g[l][None, :],
                                            ln1_b[l][None, :], wr_hi[l], wr_lo[l], b_r[l][None, :])
        block_e, n_valid_blocks, tail, pos = _dispatch_plan(route, counts, n_blocks)
        xs = _dispatch(pos, tail, n_valid_blocks, h1, n_blocks)
        ys = _experts(l, block_e, n_valid_blocks, xs, w_gate, w_up, w_down)
        h = _combine_norm(alpha, pos, ys, h1, route, ln2_g[l][None, :], ln2_b[l][None, :])
    return h.reshape(B, S, D)
```

```python
import functools

import jax
import jax.numpy as jnp
from jax import lax
from jax.experimental import pallas as pl
from jax.experimental.pallas import tpu as pltpu

F32 = jnp.float32
BF16 = jnp.bfloat16

POOL_WINDOWS = (2, 4, 8, 16)
SB_HEAD_DIM = 64
N_EXPERT_GROUPS = 4
EXPERTS_PER_GROUP = 8
N_EXPERTS = N_EXPERT_GROUPS * EXPERTS_PER_GROUP
TOP_K = 2
LN_EPS = 1e-5

LANES = 128
SUBLANES = 8
ROW_TILE = 512
QUERY_BLOCK = 256
KEY_BLOCK = 128
KEY_SEG = KEY_BLOCK // SUBLANES
DISPATCH_ROWS = 256
DISPATCH_TILE = 256
COMBINE_ROWS = 256
NEG_BIG = -1e30
VMEM_LIMIT = 48 * 1024 * 1024


def _params(sem):
    return pltpu.CompilerParams(dimension_semantics=sem, vmem_limit_bytes=VMEM_LIMIT)


def _proj_kernel(x_ref, wu_ref, wqt_ref, wk_ref, wvt_ref, u_ref, qt_ref, k_ref, vt_ref):
    rows = x_ref.shape[0]
    nt = (((1,), (1,)), ((), ()))
    xb = x_ref[...].astype(BF16)
    u_ref[...] = jnp.dot(xb, wu_ref[...], preferred_element_type=F32)
    qt_ref[...] = lax.dot_general(wqt_ref[...], xb, nt,
                                  preferred_element_type=F32).astype(qt_ref.dtype)
    p_i = lax.broadcasted_iota(jnp.int32, (KEY_BLOCK, KEY_BLOCK), 0)
    t_i = lax.broadcasted_iota(jnp.int32, (KEY_BLOCK, KEY_BLOCK), 1)
    src = (p_i % SUBLANES) * KEY_SEG + p_i // SUBLANES
    perm = jnp.where(t_i == src, 1.0, 0.0).astype(BF16)
    xp = jnp.concatenate(
        [jnp.dot(perm, xb[g * KEY_BLOCK:(g + 1) * KEY_BLOCK, :], preferred_element_type=F32)
         for g in range(rows // KEY_BLOCK)], axis=0).astype(BF16)
    k_ref[...] = jnp.dot(xp, wk_ref[...], preferred_element_type=F32).astype(k_ref.dtype)
    vt_ref[...] = lax.dot_general(wvt_ref[...], xp, nt,
                                  preferred_element_type=F32).astype(vt_ref.dtype)


def _project(h2d, w_u, w_qt, w_k, w_vt):
    T, D = h2d.shape
    width = w_u.shape[1]
    row_major = lambda: pl.BlockSpec((ROW_TILE, width), lambda i: (i, 0))
    feat_major = lambda: pl.BlockSpec((width, ROW_TILE), lambda i: (0, i))
    full = lambda a: pl.BlockSpec(a.shape, lambda i: (0, 0))
    return pl.pallas_call(
        _proj_kernel,
        grid=(T // ROW_TILE,),
        in_specs=[pl.BlockSpec((ROW_TILE, D), lambda i: (i, 0)),
                  full(w_u), full(w_qt), full(w_k), full(w_vt)],
        out_specs=[row_major(), feat_major(), row_major(), feat_major()],
        out_shape=[jax.ShapeDtypeStruct((T, width), F32),
                   jax.ShapeDtypeStruct((width, T), BF16),
                   jax.ShapeDtypeStruct((T, width), BF16),
                   jax.ShapeDtypeStruct((width, T), BF16)],
        compiler_params=_params(("parallel",)),
        name="proj",
    )(h2d, w_u, w_qt, w_k, w_vt)


def _pool_kernel(u_ref, w_ref, s_ref, o_ref):
    S = u_ref.shape[0]
    row = lax.broadcasted_iota(jnp.int32, (S, LANES), 0)
    for g, win in enumerate(POOL_WINDOWS):
        u = u_ref[:, g * LANES:(g + 1) * LANES]
        acc = u
        k = 1
        while k < win:
            shifted = jnp.where(row >= k, pltpu.roll(acc, k, 0), 0.0)
            acc = acc + shifted
            k *= 2
        count = jnp.minimum(row + 1, win).astype(F32)
        pooled = acc / count - u
        mixed = jnp.dot(pooled.astype(BF16), w_ref[g], preferred_element_type=F32)
        o_ref[:, g * LANES:(g + 1) * LANES] = (
            mixed * s_ref[:, g * LANES:(g + 1) * LANES]).astype(o_ref.dtype)


def _pool(u, w_pool_b, pool_scale, B, S):
    T, W = u.shape
    return pl.pallas_call(
        _pool_kernel,
        grid=(B,),
        in_specs=[pl.BlockSpec((S, W), lambda b: (b, 0)),
                  pl.BlockSpec(w_pool_b.shape, lambda b: (0, 0, 0)),
                  pl.BlockSpec((1, W), lambda b: (0, 0))],
        out_specs=pl.BlockSpec((S, W), lambda b: (b, 0)),
        out_shape=jax.ShapeDtypeStruct((T, W), BF16),
        compiler_params=_params(("parallel",)),
        name="pool",
    )(u, w_pool_b, pool_scale)


def _attn_kernel(qt_ref, k_ref, vt_ref, o_ref, acc_ref, car_ref):
    i = pl.program_id(2)
    qb = QUERY_BLOCK
    kb = KEY_BLOCK
    width = 2 * qb
    q2 = qt_ref[...]
    feat = lax.broadcasted_iota(jnp.int32, q2.shape, 0)
    zero = jnp.zeros_like(q2)
    rhs = jnp.concatenate([jnp.where(feat < SB_HEAD_DIM, q2, zero),
                           jnp.where(feat >= SB_HEAD_DIM, q2, zero)], axis=1)
    sub = lax.broadcasted_iota(jnp.int32, (SUBLANES, width), 0)
    lane = lax.broadcasted_iota(jnp.int32, (SUBLANES, width), 1)
    q_pos = i * qb + jnp.bitwise_and(lane, qb - 1)

    acc_ref[...] = jnp.zeros_like(acc_ref)
    car_ref[...] = jnp.zeros_like(car_ref)

    def block(j, masked):
        start = pl.multiple_of(j * kb, kb)
        keys = k_ref[pl.ds(start, kb), :]
        vals = vt_ref[:, pl.ds(start, kb)]
        z = jnp.dot(keys, rhs, preferred_element_type=F32)
        zs, sps, sufs, masks = [], [], [], []
        run = jnp.zeros((SUBLANES, width), F32)
        for v in reversed(range(KEY_SEG)):
            zv = z[v * SUBLANES:(v + 1) * SUBLANES, :]
            sp = jnp.maximum(zv, 0.0) + jnp.log(1.0 + jnp.exp(-jnp.abs(zv)))
            if masked:
                m = (start + sub * KEY_SEG + v) < q_pos
                sp = jnp.where(m, sp, 0.0)
                masks.append(m)
            zs.append(zv)
            sps.append(sp)
            sufs.append(run)
            run = run + sp
        incl = run
        for step in (1, 2, 4):
            shifted = pltpu.roll(incl, SUBLANES - step, 0)
            incl = incl + jnp.where(sub < SUBLANES - step, shifted, 0.0)
        base = (incl - run) + car_ref[...]
        parts = []
        for idx in range(KEY_SEG):
            a = jnp.exp(zs[idx] - sps[idx] - (sufs[idx] + base))
            if masked:
                a = jnp.where(masks[idx], a, 0.0)
            parts.append(a)
        a_t = jnp.concatenate(parts[::-1], axis=0).astype(BF16)
        acc_ref[...] += jnp.dot(vals, a_t, preferred_element_type=F32)
        car_ref[...] += jnp.broadcast_to(incl[0:1, :], (SUBLANES, width))

    n_diag = qb // kb

    def masked_body(t, carry):
        block((i + 1) * n_diag - 1 - t, True)
        return carry

    def full_body(t, carry):
        block(i * n_diag - 1 - t, False)
        return carry

    lax.fori_loop(0, n_diag, masked_body, 0)
    lax.fori_loop(0, i * n_diag, full_body, 0)

    acc = acc_ref[...]
    feat_o = lax.broadcasted_iota(jnp.int32, (acc.shape[0], qb), 0)
    both = jnp.where(feat_o < SB_HEAD_DIM, acc[:, :qb], acc[:, qb:])
    o_ref[...] = both.T.astype(o_ref.dtype)


def _attention(q_t, k_perm, v_t, B, S):
    W, T = q_t.shape
    qb = QUERY_BLOCK
    nq = S // qb
    pairs = W // LANES
    return pl.pallas_call(
        _attn_kernel,
        grid=(B, pairs, nq),
        in_specs=[pl.BlockSpec((LANES, qb), lambda b, p, i: (p, b * nq + i)),
                  pl.BlockSpec((S, LANES), lambda b, p, i: (b, p)),
                  pl.BlockSpec((LANES, S), lambda b, p, i: (p, b))],
        out_specs=pl.BlockSpec((qb, LANES), lambda b, p, i: (b * nq + i, p)),
        out_shape=jax.ShapeDtypeStruct((T, W), BF16),
        scratch_shapes=[pltpu.VMEM((LANES, 2 * qb), F32), pltpu.VMEM((SUBLANES, 2 * qb), F32)],
        compiler_params=_params(("parallel", "parallel", "parallel")),
        name="attn",
    )(q_t, k_perm, v_t)


def _layer_norm(y, g, b):
    mu = jnp.mean(y, axis=-1, keepdims=True)
    yc = y - mu
    var = jnp.mean(yc * yc, axis=-1, keepdims=True)
    return yc * lax.rsqrt(var + LN_EPS) * g + b


def _mix_kernel(alpha, yp_ref, ys_ref, h_ref, wo_ref, g_ref, b_ref, wrh_ref, wrl_ref, br_ref,
                h1_ref, route_ref, cnt_ref, run_ref):
    @pl.when(pl.program_id(0) == 0)
    def _():
        run_ref[...] = jnp.zeros_like(run_ref)

    half = yp_ref.shape[1]
    mix = (jnp.dot(yp_ref[...], wo_ref[:half, :], preferred_element_type=F32) +
           jnp.dot(ys_ref[...], wo_ref[half:, :], preferred_element_type=F32))
    h1 = _layer_norm(alpha * h_ref[...] + mix, g_ref[...], b_ref[...])
    h1_ref[...] = h1

    hh = h1.astype(BF16)
    hl = (h1 - hh.astype(F32)).astype(BF16)
    logits = (jnp.dot(hh, wrh_ref[...], preferred_element_type=F32) +
              jnp.dot(hl, wrh_ref[...], preferred_element_type=F32) +
              jnp.dot(hh, wrl_ref[...], preferred_element_type=F32) + br_ref[...])
    col = lax.broadcasted_iota(jnp.int32, logits.shape, 1)
    gl = jnp.where(col < N_EXPERT_GROUPS, logits, NEG_BIG)
    gmax = jnp.max(gl, axis=-1, keepdims=True)
    gidx = jnp.min(jnp.where(gl == gmax, col, LANES), axis=-1, keepdims=True)
    g_p = 1.0 / jnp.sum(jnp.exp(gl - gmax), axis=-1, keepdims=True)
    lo_col = N_EXPERT_GROUPS + EXPERTS_PER_GROUP * gidx
    el = jnp.where(col >= lo_col, jnp.where(col < lo_col + EXPERTS_PER_GROUP, logits, NEG_BIG),
                   NEG_BIG)
    m1 = jnp.max(el, axis=-1, keepdims=True)
    i1 = jnp.min(jnp.where(el == m1, col, LANES), axis=-1, keepdims=True)
    el2 = jnp.where(col == i1, NEG_BIG, el)
    m2 = jnp.max(el2, axis=-1, keepdims=True)
    i2 = jnp.min(jnp.where(el2 == m2, col, LANES), axis=-1, keepdims=True)
    ratio = jnp.exp(m2 - m1)
    gate1 = g_p / (1.0 + ratio)
    gate2 = g_p * ratio / (1.0 + ratio)
    e1 = (i1 - N_EXPERT_GROUPS).astype(F32)
    e2 = (i2 - N_EXPERT_GROUPS).astype(F32)

    rows = logits.shape[0]
    hit1 = col == i1
    hit2 = col == i2
    onehot = jnp.where(hit1, 1.0, jnp.where(hit2, 1.0, 0.0))
    r_i = lax.broadcasted_iota(jnp.int32, (rows, rows), 0)
    c_i = lax.broadcasted_iota(jnp.int32, (rows, rows), 1)
    earlier = jnp.where(c_i < r_i, 1.0, 0.0).astype(BF16)
    before = run_ref[...] + jnp.dot(earlier, onehot.astype(BF16), preferred_element_type=F32)
    rank1 = jnp.sum(jnp.where(hit1, before, 0.0), axis=-1, keepdims=True)
    rank2 = jnp.sum(jnp.where(hit2, before, 0.0), axis=-1, keepdims=True)
    run_ref[...] += jnp.sum(onehot, axis=0, keepdims=True)
    cnt_ref[...] = run_ref[...]

    route_ref[...] = jnp.where(col == 0, e1, jnp.where(col == 1, e2,
                               jnp.where(col == 2, gate1, jnp.where(col == 3, gate2,
                               jnp.where(col == 4, rank1, jnp.where(col == 5, rank2, 0.0))))))


def _mix_norm_route(alpha, y_pool, y_sb, h2d, w_out_b, g, b, wr_hi, wr_lo, b_r):
    T, D = h2d.shape
    half = y_pool.shape[1]
    full = lambda a: pl.BlockSpec(a.shape, lambda i: (0,) * a.ndim)
    return pl.pallas_call(
        functools.partial(_mix_kernel, alpha),
        grid=(T // ROW_TILE,),
        in_specs=[pl.BlockSpec((ROW_TILE, half), lambda i: (i, 0)),
                  pl.BlockSpec((ROW_TILE, half), lambda i: (i, 0)),
                  pl.BlockSpec((ROW_TILE, D), lambda i: (i, 0)),
                  full(w_out_b), full(g), full(b), full(wr_hi), full(wr_lo), full(b_r)],
        out_specs=[pl.BlockSpec((ROW_TILE, D), lambda i: (i, 0)),
                   pl.BlockSpec((ROW_TILE, LANES), lambda i: (i, 0)),
                   pl.BlockSpec((1, LANES), lambda i: (0, 0))],
        out_shape=[jax.ShapeDtypeStruct((T, D), F32), jax.ShapeDtypeStruct((T, LANES), F32),
                   jax.ShapeDtypeStruct((1, LANES), F32)],
        scratch_shapes=[pltpu.VMEM((1, LANES), F32)],
        compiler_params=_params(("arbitrary",)),
        name="mix_norm_route",
    )(y_pool, y_sb, h2d, w_out_b, g, b, wr_hi, wr_lo, b_r)


def _row_gather_copy(src_hbm, row, dst_ref, dst_row, sem):
    return pltpu.make_async_copy(src_hbm.at[pl.ds(row, 1)], dst_ref.at[pl.ds(dst_row, 1)], sem)


def _row_scatter_copy(src_ref, src_row, dst_hbm, row, sem):
    return pltpu.make_async_copy(src_ref.at[pl.ds(src_row, 1)], dst_hbm.at[pl.ds(row, 1)], sem)


def _dispatch_kernel(pos_ref, tail_ref, nvb_ref, h_ref, xs_hbm, sbuf, zbuf, sem, zsem):
    i = pl.program_id(0)
    nb = pl.num_programs(0)
    rows = DISPATCH_TILE
    blk = DISPATCH_ROWS
    n_blocks = xs_hbm.shape[0] // blk

    def zero_copy(start):
        return pltpu.make_async_copy(zbuf, xs_hbm.at[pl.ds(pl.multiple_of(start, blk), blk)], zsem)

    @pl.when(i == 0)
    def _():
        zbuf[...] = jnp.zeros_like(zbuf)
        for e in range(N_EXPERTS):
            @pl.when(tail_ref[e] >= 0)
            def _():
                zero_copy(tail_ref[e]).start()

        def start_unused(b, carry):
            zero_copy(b * blk).start()
            return carry

        def wait_unused(b, carry):
            zero_copy(b * blk).wait()
            return carry

        lax.fori_loop(nvb_ref[0], n_blocks, start_unused, 0)
        for e in range(N_EXPERTS):
            @pl.when(tail_ref[e] >= 0)
            def _():
                zero_copy(tail_ref[e]).wait()
        lax.fori_loop(nvb_ref[0], n_blocks, wait_unused, 0)

    slot = i % 2

    def wait_slot(s):
        for _ in range(TOP_K):
            pltpu.make_async_copy(sbuf.at[s], xs_hbm.at[pl.ds(0, rows)], sem.at[s]).wait()

    @pl.when(i >= 2)
    def _():
        wait_slot(slot)

    sbuf[slot] = h_ref[...]

    def body(r, carry):
        for k in range(TOP_K):
            _row_scatter_copy(sbuf.at[slot], r, xs_hbm, pos_ref[(i * rows + r) * TOP_K + k],
                              sem.at[slot]).start()
        return carry

    lax.fori_loop(0, rows, body, 0, unroll=8)

    @pl.when(i == nb - 1)
    def _():
        @pl.when(nb >= 2)
        def _():
            wait_slot(1 - slot)
        wait_slot(slot)


def _dispatch(pos, tail, n_valid_blocks, h1, n_blocks):
    T, D = h1.shape
    rows = DISPATCH_TILE
    grid_spec = pltpu.PrefetchScalarGridSpec(
        num_scalar_prefetch=3,
        grid=(T // rows,),
        in_specs=[pl.BlockSpec((rows, D), lambda i, pos, tail, nvb: (i, 0))],
        out_specs=pl.BlockSpec(memory_space=pl.ANY),
        scratch_shapes=[pltpu.VMEM((2, rows, D), F32), pltpu.VMEM((DISPATCH_ROWS, D), F32),
                        pltpu.SemaphoreType.DMA((2,)), pltpu.SemaphoreType.DMA],
    )
    return pl.pallas_call(
        _dispatch_kernel,
        grid_spec=grid_spec,
        out_shape=jax.ShapeDtypeStruct((n_blocks * DISPATCH_ROWS, D), F32),
        compiler_params=_params(("arbitrary",)),
        name="dispatch",
    )(pos, tail, n_valid_blocks, h1)


def _expert_kernel(be_ref, nvb_ref, xs_ref, wg_ref, wu_ref, wd_ref, ys_ref, wgb, wub, wdb):
    i = pl.program_id(0)
    e = be_ref[i]
    prev = be_ref[jnp.maximum(i - 1, 0)]

    @pl.when(jnp.logical_or(i == 0, e != prev))
    def _():
        wgb[...] = wg_ref[...].astype(BF16)
        wub[...] = wu_ref[...].astype(BF16)
        wdb[...] = wd_ref[...].astype(BF16)

    @pl.when(i < nvb_ref[0])
    def _():
        x = xs_ref[...].astype(BF16)
        gate = jnp.dot(x, wgb[...], preferred_element_type=F32)
        up = jnp.dot(x, wub[...], preferred_element_type=F32)
        hid = (gate * jax.nn.sigmoid(gate) * up).astype(BF16)
        ys_ref[...] = jnp.dot(hid, wdb[...], preferred_element_type=F32)

    @pl.when(i >= nvb_ref[0])
    def _():
        ys_ref[...] = jnp.zeros_like(ys_ref)


def _experts(layer, block_e, n_valid_blocks, xs, w_gate, w_up, w_down):
    P, D = xs.shape
    DE = w_gate.shape[-1]
    rows = DISPATCH_ROWS
    n_blocks = P // rows
    x_map = lambda i, be, nvb: (jnp.minimum(i, jnp.maximum(nvb[0] - 1, 0)), 0)
    grid_spec = pltpu.PrefetchScalarGridSpec(
        num_scalar_prefetch=2,
        grid=(n_blocks,),
        in_specs=[pl.BlockSpec((rows, D), x_map),
                  pl.BlockSpec((None, None, D, DE), lambda i, be, nvb: (layer, be[i], 0, 0)),
                  pl.BlockSpec((None, None, D, DE), lambda i, be, nvb: (layer, be[i], 0, 0)),
                  pl.BlockSpec((None, None, DE, D), lambda i, be, nvb: (layer, be[i], 0, 0))],
        out_specs=pl.BlockSpec((rows, D), lambda i, be, nvb: (i, 0)),
        scratch_shapes=[pltpu.VMEM((D, DE), BF16), pltpu.VMEM((D, DE), BF16),
                        pltpu.VMEM((DE, D), BF16)],
    )
    return pl.pallas_call(
        _expert_kernel,
        grid_spec=grid_spec,
        out_shape=jax.ShapeDtypeStruct((P, D), F32),
        compiler_params=_params(("arbitrary",)),
        name="experts",
    )(block_e, n_valid_blocks, xs, w_gate, w_up, w_down)


def _combine_kernel(alpha, pos_ref, ys_hbm, h_ref, route_ref, g_ref, b_ref, o_ref, ybuf, sem):
    i = pl.program_id(0)
    nb = pl.num_programs(0)
    rows = COMBINE_ROWS

    def issue(blk, slot):
        def body(r, carry):
            for k in range(TOP_K):
                _row_gather_copy(ys_hbm, pos_ref[(blk * rows + r) * TOP_K + k],
                                 ybuf.at[slot, k], r, sem.at[slot]).start()
            return carry
        lax.fori_loop(0, rows, body, 0, unroll=8)

    @pl.when(i == 0)
    def _():
        issue(0, 0)

    @pl.when(i + 1 < nb)
    def _():
        issue(i + 1, (i + 1) % 2)

    slot = i % 2
    for k in range(TOP_K):
        pltpu.make_async_copy(ys_hbm.at[pl.ds(0, rows)], ybuf.at[slot, k], sem.at[slot]).wait()
    route = route_ref[...]
    moe = route[:, 2:3] * ybuf[slot, 0] + route[:, 3:4] * ybuf[slot, 1]
    o_ref[...] = _layer_norm(alpha * h_ref[...] + moe, g_ref[...], b_ref[...])


def _combine_norm(alpha, pos, ys, h1, route, g, b):
    T, D = h1.shape
    rows = COMBINE_ROWS
    grid_spec = pltpu.PrefetchScalarGridSpec(
        num_scalar_prefetch=1,
        grid=(T // rows,),
        in_specs=[pl.BlockSpec(memory_space=pl.ANY),
                  pl.BlockSpec((rows, D), lambda i, pos: (i, 0)),
                  pl.BlockSpec((rows, LANES), lambda i, pos: (i, 0)),
                  pl.BlockSpec((1, D), lambda i, pos: (0, 0)),
                  pl.BlockSpec((1, D), lambda i, pos: (0, 0))],
        out_specs=pl.BlockSpec((rows, D), lambda i, pos: (i, 0)),
        scratch_shapes=[pltpu.VMEM((2, TOP_K, rows, D), F32), pltpu.SemaphoreType.DMA((2,))],
    )
    return pl.pallas_call(
        functools.partial(_combine_kernel, alpha),
        grid_spec=grid_spec,
        out_shape=jax.ShapeDtypeStruct((T, D), F32),
        compiler_params=_params(("arbitrary",)),
        name="combine_norm",
    )(pos, ys, h1, route, g, b)


def _dispatch_plan(route, counts_row, n_blocks):
    rows = DISPATCH_ROWS
    experts = jnp.arange(N_EXPERTS, dtype=jnp.int32)
    counts = counts_row[0, N_EXPERT_GROUPS:N_EXPERT_GROUPS + N_EXPERTS].astype(jnp.int32)
    padded = ((counts + rows - 1) // rows) * rows
    pend = jnp.cumsum(padded)
    pstart = pend - padded
    block_start = jnp.arange(n_blocks, dtype=jnp.int32) * rows
    block_e = jnp.minimum(jnp.sum((block_start[:, None] >= pend[None, :]).astype(jnp.int32), axis=1),
                          N_EXPERTS - 1).astype(jnp.int32)
    n_valid_blocks = (pend[-1:] // rows).astype(jnp.int32)
    tail = jnp.where(counts > 0, pend - rows, -1).astype(jnp.int32)
    e = route[:, :TOP_K].astype(jnp.int32)
    rank = route[:, 4:4 + TOP_K].astype(jnp.int32)
    base = jnp.sum(jnp.where(e[:, :, None] == experts[None, None, :], pstart[None, None, :], 0),
                   axis=-1)
    pos = (base + rank).reshape(-1).astype(jnp.int32)
    return block_e, n_valid_blocks, tail, pos


def kernel(x, w_in, w_pool, pool_scale, w_out, ln1_g, ln1_b, w_router_group, b_router_group,
           w_router_expert, b_router_expert, w_gate, w_up, w_down, ln2_g, ln2_b):
    B, S, D = x.shape
    depth = w_in.shape[0]
    T = B * S
    alpha = (2.0 * depth) ** 0.25
    pool_width = w_pool.shape[1] * w_pool.shape[2]
    sb_width = (w_in.shape[2] - pool_width) // 3
    n_blocks = -(-T * TOP_K // DISPATCH_ROWS) + N_EXPERTS

    col_scale = jnp.concatenate([jnp.ones((pool_width,), F32),
                                 jnp.full((sb_width,), SB_HEAD_DIM ** -0.5, F32),
                                 jnp.ones((2 * sb_width,), F32)])
    w_in_s = w_in * col_scale
    w_u = w_in_s[:, :, :pool_width].astype(BF16)
    w_qt = jnp.swapaxes(w_in_s[:, :, pool_width:pool_width + sb_width], 1, 2).astype(BF16)
    w_k = w_in_s[:, :, pool_width + sb_width:pool_width + 2 * sb_width].astype(BF16)
    w_vt = jnp.swapaxes(w_in_s[:, :, pool_width + 2 * sb_width:], 1, 2).astype(BF16)
    w_pool_b = w_pool.astype(BF16)
    w_out_b = w_out.astype(BF16)
    w_r = jnp.concatenate([w_router_group, w_router_expert,
                           jnp.zeros((depth, D, LANES - N_EXPERT_GROUPS - N_EXPERTS), F32)], axis=-1)
    wr_hi = w_r.astype(BF16)
    wr_lo = (w_r - wr_hi.astype(F32)).astype(BF16)
    b_r = jnp.concatenate([b_router_group.astype(F32),
                           b_router_expert.astype(F32).reshape(depth, N_EXPERTS),
                           jnp.zeros((depth, LANES - N_EXPERT_GROUPS - N_EXPERTS), F32)], axis=-1)

    h = x.reshape(T, D)
    for l in range(depth):
        u, q_t, k_perm, v_t = _project(h, w_u[l], w_qt[l], w_k[l], w_vt[l])
        y_pool = _pool(u, w_pool_b[l], pool_scale[l][None, :], B, S)
        y_sb = _attention(q_t, k_perm, v_t, B, S)
        h1, route, counts = _mix_norm_route(alpha, y_pool, y_sb, h, w_out_b[l], ln1_g[l][None, :],
                                            ln1_b[l][None, :], wr_hi[l], wr_lo[l], b_r[l][None, :])
        block_e, n_valid_blocks, tail, pos = _dispatch_plan(route, counts, n_blocks)
        xs = _dispatch(pos, tail, n_valid_blocks, h1, n_blocks)
        ys = _experts(l, block_e, n_valid_blocks, xs, w_gate, w_up, w_down)
        h = _combine_norm(alpha, pos, ys, h1, route, ln2_g[l][None, :], ln2_b[l][None, :])
    return h.reshape(B, S, D)
```

```python
import functools

import jax
import jax.numpy as jnp
from jax import lax
from jax.experimental import pallas as pl
from jax.experimental.pallas import tpu as pltpu

F32 = jnp.float32
BF16 = jnp.bfloat16

POOL_WINDOWS = (2, 4, 8, 16)
SB_HEAD_DIM = 64
N_EXPERT_GROUPS = 4
EXPERTS_PER_GROUP = 8
N_EXPERTS = N_EXPERT_GROUPS * EXPERTS_PER_GROUP
TOP_K = 2
LN_EPS = 1e-5

LANES = 128
SUBLANES = 8
ROW_TILE = 512
QUERY_BLOCK = 256
KEY_BLOCK = 128
KEY_SEG = KEY_BLOCK // SUBLANES
DISPATCH_ROWS = 256
DISPATCH_TILE = 256
COMBINE_ROWS = 256
NEG_BIG = -1e30
VMEM_LIMIT = 48 * 1024 * 1024


def _params(sem):
    return pltpu.CompilerParams(dimension_semantics=sem, vmem_limit_bytes=VMEM_LIMIT)


def _proj_kernel(x_ref, wu_ref, wqt_ref, wk_ref, wvt_ref, u_ref, qt_ref, k_ref, vt_ref):
    rows = x_ref.shape[0]
    nt = (((1,), (1,)), ((), ()))
    xb = x_ref[...].astype(BF16)
    u_ref[...] = jnp.dot(xb, wu_ref[...], preferred_element_type=F32)
    qt_ref[...] = lax.dot_general(wqt_ref[...], xb, nt,
                                  preferred_element_type=F32).astype(qt_ref.dtype)
    p_i = lax.broadcasted_iota(jnp.int32, (KEY_BLOCK, KEY_BLOCK), 0)
    t_i = lax.broadcasted_iota(jnp.int32, (KEY_BLOCK, KEY_BLOCK), 1)
    src = (p_i % SUBLANES) * KEY_SEG + p_i // SUBLANES
    perm = jnp.where(t_i == src, 1.0, 0.0).astype(BF16)
    xp = jnp.concatenate(
        [jnp.dot(perm, xb[g * KEY_BLOCK:(g + 1) * KEY_BLOCK, :], preferred_element_type=F32)
         for g in range(rows // KEY_BLOCK)], axis=0).astype(BF16)
    k_ref[...] = jnp.dot(xp, wk_ref[...], preferred_element_type=F32).astype(k_ref.dtype)
    vt_ref[...] = lax.dot_general(wvt_ref[...], xp, nt,
                                  preferred_element_type=F32).astype(vt_ref.dtype)


def _project(h2d, w_u, w_qt, w_k, w_vt):
    T, D = h2d.shape
    width = w_u.shape[1]
    row_major = lambda: pl.BlockSpec((ROW_TILE, width), lambda i: (i, 0))
    feat_major = lambda: pl.BlockSpec((width, ROW_TILE), lambda i: (0, i))
    full = lambda a: pl.BlockSpec(a.shape, lambda i: (0, 0))
    return pl.pallas_call(
        _proj_kernel,
        grid=(T // ROW_TILE,),
        in_specs=[pl.BlockSpec((ROW_TILE, D), lambda i: (i, 0)),
                  full(w_u), full(w_qt), full(w_k), full(w_vt)],
        out_specs=[row_major(), feat_major(), row_major(), feat_major()],
        out_shape=[jax.ShapeDtypeStruct((T, width), F32),
                   jax.ShapeDtypeStruct((width, T), BF16),
                   jax.ShapeDtypeStruct((T, width), BF16),
                   jax.ShapeDtypeStruct((width, T), BF16)],
        compiler_params=_params(("parallel",)),
        name="proj",
    )(h2d, w_u, w_qt, w_k, w_vt)


def _pool_kernel(u_ref, w_ref, s_ref, o_ref):
    S = u_ref.shape[0]
    row = lax.broadcasted_iota(jnp.int32, (S, LANES), 0)
    for g, win in enumerate(POOL_WINDOWS):
        u = u_ref[:, g * LANES:(g + 1) * LANES]
        acc = u
        k = 1
        while k < win:
            shifted = jnp.where(row >= k, pltpu.roll(acc, k, 0), 0.0)
            acc = acc + shifted
            k *= 2
        count = jnp.minimum(row + 1, win).astype(F32)
        pooled = acc / count - u
        mixed = jnp.dot(pooled.astype(BF16), w_ref[g], preferred_element_type=F32)
        o_ref[:, g * LANES:(g + 1) * LANES] = (
            mixed * s_ref[:, g * LANES:(g + 1) * LANES]).astype(o_ref.dtype)


def _pool(u, w_pool_b, pool_scale, B, S):
    T, W = u.shape
    return pl.pallas_call(
        _pool_kernel,
        grid=(B,),
        in_specs=[pl.BlockSpec((S, W), lambda b: (b, 0)),
                  pl.BlockSpec(w_pool_b.shape, lambda b: (0, 0, 0)),
                  pl.BlockSpec((1, W), lambda b: (0, 0))],
        out_specs=pl.BlockSpec((S, W), lambda b: (b, 0)),
        out_shape=jax.ShapeDtypeStruct((T, W), BF16),
        compiler_params=_params(("parallel",)),
        name="pool",
    )(u, w_pool_b, pool_scale)


def _attn_kernel(qt_ref, k_ref, vt_ref, o_ref, acc_ref, car_ref, z0_ref, z1_ref, a0_ref, a1_ref):
    i = pl.program_id(2)
    qb = QUERY_BLOCK
    kb = KEY_BLOCK
    width = 2 * qb
    q2 = qt_ref[...]
    feat = lax.broadcasted_iota(jnp.int32, q2.shape, 0)
    zero = jnp.zeros_like(q2)
    rhs = jnp.concatenate([jnp.where(feat < SB_HEAD_DIM, q2, zero),
                           jnp.where(feat >= SB_HEAD_DIM, q2, zero)], axis=1)
    sub = lax.broadcasted_iota(jnp.int32, (SUBLANES, width), 0)
    lane = lax.broadcasted_iota(jnp.int32, (SUBLANES, width), 1)
    q_pos = i * qb + jnp.bitwise_and(lane, qb - 1)

    acc_ref[...] = jnp.zeros_like(acc_ref)
    car_ref[...] = jnp.zeros_like(car_ref)

    n_diag = qb // kb
    last = (i + 1) * n_diag - 1

    def block_start(n):
        return pl.multiple_of(jnp.clip(last - n, 0, last) * kb, kb)

    def scores(n, z_ref):
        z_ref[...] = jnp.dot(k_ref[pl.ds(block_start(n), kb), :], rhs,
                             preferred_element_type=F32)

    def values(n, a_ref):
        acc_ref[...] += jnp.dot(vt_ref[:, pl.ds(block_start(n), kb)], a_ref[...],
                                preferred_element_type=F32)

    def weights(n, z_ref, a_ref, masked):
        start = block_start(n)
        zs, sps, sufs, masks = [], [], [], []
        run = jnp.zeros((SUBLANES, width), F32)
        for v in reversed(range(KEY_SEG)):
            zv = z_ref[v * SUBLANES:(v + 1) * SUBLANES, :]
            sp = jnp.maximum(zv, 0.0) + jnp.log(1.0 + jnp.exp(-jnp.abs(zv)))
            if masked:
                m = (start + sub * KEY_SEG + v) < q_pos
                sp = jnp.where(m, sp, 0.0)
                masks.append(m)
            zs.append(zv)
            sps.append(sp)
            sufs.append(run)
            run = run + sp
        incl = run
        for step in (1, 2, 4):
            shifted = pltpu.roll(incl, SUBLANES - step, 0)
            incl = incl + jnp.where(sub < SUBLANES - step, shifted, 0.0)
        base = (incl - run) + car_ref[...]
        parts = []
        for idx in range(KEY_SEG):
            a = jnp.exp(zs[idx] - sps[idx] - (sufs[idx] + base))
            if masked:
                a = jnp.where(masks[idx], a, 0.0)
            parts.append(a)
        a_ref[...] = jnp.concatenate(parts[::-1], axis=0).astype(BF16)
        car_ref[...] += jnp.broadcast_to(incl[0:1, :], (SUBLANES, width))

    assert n_diag == 2
    scores(0, z0_ref)
    scores(1, z1_ref)
    weights(0, z0_ref, a0_ref, True)
    scores(2, z0_ref)
    weights(1, z1_ref, a1_ref, True)
    values(0, a0_ref)

    def pair(t, carry):
        n = 2 + 2 * t
        values(n - 1, a1_ref)
        weights(n, z0_ref, a0_ref, False)
        scores(n + 1, z1_ref)
        values(n, a0_ref)
        weights(n + 1, z1_ref, a1_ref, False)
        scores(n + 2, z0_ref)
        return carry

    lax.fori_loop(0, i, pair, 0)
    values(last, a1_ref)

    acc = acc_ref[...]
    feat_o = lax.broadcasted_iota(jnp.int32, (acc.shape[0], qb), 0)
    both = jnp.where(feat_o < SB_HEAD_DIM, acc[:, :qb], acc[:, qb:])
    o_ref[...] = both.T.astype(o_ref.dtype)


def _attention(q_t, k_perm, v_t, B, S):
    W, T = q_t.shape
    qb = QUERY_BLOCK
    nq = S // qb
    pairs = W // LANES
    return pl.pallas_call(
        _attn_kernel,
        grid=(B, pairs, nq),
        in_specs=[pl.BlockSpec((LANES, qb), lambda b, p, i: (p, b * nq + i)),
                  pl.BlockSpec((S, LANES), lambda b, p, i: (b, p)),
                  pl.BlockSpec((LANES, S), lambda b, p, i: (p, b))],
        out_specs=pl.BlockSpec((qb, LANES), lambda b, p, i: (b * nq + i, p)),
        out_shape=jax.ShapeDtypeStruct((T, W), BF16),
        scratch_shapes=[pltpu.VMEM((LANES, 2 * qb), F32), pltpu.VMEM((SUBLANES, 2 * qb), F32),
                        pltpu.VMEM((KEY_BLOCK, 2 * qb), F32), pltpu.VMEM((KEY_BLOCK, 2 * qb), F32),
                        pltpu.VMEM((KEY_BLOCK, 2 * qb), BF16), pltpu.VMEM((KEY_BLOCK, 2 * qb), BF16)],
        compiler_params=_params(("parallel", "parallel", "parallel")),
        name="attn",
    )(q_t, k_perm, v_t)


def _layer_norm(y, g, b):
    mu = jnp.mean(y, axis=-1, keepdims=True)
    yc = y - mu
    var = jnp.mean(yc * yc, axis=-1, keepdims=True)
    return yc * lax.rsqrt(var + LN_EPS) * g + b


def _mix_kernel(alpha, yp_ref, ys_ref, h_ref, wo_ref, g_ref, b_ref, wrh_ref, wrl_ref, br_ref,
                h1_ref, route_ref, cnt_ref, run_ref):
    @pl.when(pl.program_id(0) == 0)
    def _():
        run_ref[...] = jnp.zeros_like(run_ref)

    half = yp_ref.shape[1]
    mix = (jnp.dot(yp_ref[...], wo_ref[:half, :], preferred_element_type=F32) +
           jnp.dot(ys_ref[...], wo_ref[half:, :], preferred_element_type=F32))
    h1 = _layer_norm(alpha * h_ref[...] + mix, g_ref[...], b_ref[...])
    h1_ref[...] = h1

    hh = h1.astype(BF16)
    hl = (h1 - hh.astype(F32)).astype(BF16)
    logits = (jnp.dot(hh, wrh_ref[...], preferred_element_type=F32) +
              jnp.dot(hl, wrh_ref[...], preferred_element_type=F32) +
              jnp.dot(hh, wrl_ref[...], preferred_element_type=F32) + br_ref[...])
    col = lax.broadcasted_iota(jnp.int32, logits.shape, 1)
    gl = jnp.where(col < N_EXPERT_GROUPS, logits, NEG_BIG)
    gmax = jnp.max(gl, axis=-1, keepdims=True)
    gidx = jnp.min(jnp.where(gl == gmax, col, LANES), axis=-1, keepdims=True)
    g_p = 1.0 / jnp.sum(jnp.exp(gl - gmax), axis=-1, keepdims=True)
    lo_col = N_EXPERT_GROUPS + EXPERTS_PER_GROUP * gidx
    el = jnp.where(col >= lo_col, jnp.where(col < lo_col + EXPERTS_PER_GROUP, logits, NEG_BIG),
                   NEG_BIG)
    m1 = jnp.max(el, axis=-1, keepdims=True)
    i1 = jnp.min(jnp.where(el == m1, col, LANES), axis=-1, keepdims=True)
    el2 = jnp.where(col == i1, NEG_BIG, el)
    m2 = jnp.max(el2, axis=-1, keepdims=True)
    i2 = jnp.min(jnp.where(el2 == m2, col, LANES), axis=-1, keepdims=True)
    ratio = jnp.exp(m2 - m1)
    gate1 = g_p / (1.0 + ratio)
    gate2 = g_p * ratio / (1.0 + ratio)
    e1 = (i1 - N_EXPERT_GROUPS).astype(F32)
    e2 = (i2 - N_EXPERT_GROUPS).astype(F32)

    rows = logits.shape[0]
    hit1 = col == i1
    hit2 = col == i2
    onehot = jnp.where(hit1, 1.0, jnp.where(hit2, 1.0, 0.0))
    r_i = lax.broadcasted_iota(jnp.int32, (rows, rows), 0)
    c_i = lax.broadcasted_iota(jnp.int32, (rows, rows), 1)
    earlier = jnp.where(c_i < r_i, 1.0, 0.0).astype(BF16)
    before = run_ref[...] + jnp.dot(earlier, onehot.astype(BF16), preferred_element_type=F32)
    rank1 = jnp.sum(jnp.where(hit1, before, 0.0), axis=-1, keepdims=True)
    rank2 = jnp.sum(jnp.where(hit2, before, 0.0), axis=-1, keepdims=True)
    run_ref[...] += jnp.sum(onehot, axis=0, keepdims=True)
    cnt_ref[...] = run_ref[...]

    route_ref[...] = jnp.where(col == 0, e1, jnp.where(col == 1, e2,
                               jnp.where(col == 2, gate1, jnp.where(col == 3, gate2,
                               jnp.where(col == 4, rank1, jnp.where(col == 5, rank2, 0.0))))))


def _mix_norm_route(alpha, y_pool, y_sb, h2d, w_out_b, g, b, wr_hi, wr_lo, b_r):
    T, D = h2d.shape
    half = y_pool.shape[1]
    full = lambda a: pl.BlockSpec(a.shape, lambda i: (0,) * a.ndim)
    return pl.pallas_call(
        functools.partial(_mix_kernel, alpha),
        grid=(T // ROW_TILE,),
        in_specs=[pl.BlockSpec((ROW_TILE, half), lambda i: (i, 0)),
                  pl.BlockSpec((ROW_TILE, half), lambda i: (i, 0)),
                  pl.BlockSpec((ROW_TILE, D), lambda i: (i, 0)),
                  full(w_out_b), full(g), full(b), full(wr_hi), full(wr_lo), full(b_r)],
        out_specs=[pl.BlockSpec((ROW_TILE, D), lambda i: (i, 0)),
                   pl.BlockSpec((ROW_TILE, LANES), lambda i: (i, 0)),
                   pl.BlockSpec((1, LANES), lambda i: (0, 0))],
        out_shape=[jax.ShapeDtypeStruct((T, D), F32), jax.ShapeDtypeStruct((T, LANES), F32),
                   jax.ShapeDtypeStruct((1, LANES), F32)],
        scratch_shapes=[pltpu.VMEM((1, LANES), F32)],
        compiler_params=_params(("arbitrary",)),
        name="mix_norm_route",
    )(y_pool, y_sb, h2d, w_out_b, g, b, wr_hi, wr_lo, b_r)


def _row_gather_copy(src_hbm, row, dst_ref, dst_row, sem):
    return pltpu.make_async_copy(src_hbm.at[pl.ds(row, 1)], dst_ref.at[pl.ds(dst_row, 1)], sem)


def _row_scatter_copy(src_ref, src_row, dst_hbm, row, sem):
    return pltpu.make_async_copy(src_ref.at[pl.ds(src_row, 1)], dst_hbm.at[pl.ds(row, 1)], sem)


def _dispatch_kernel(pos_ref, tail_ref, nvb_ref, h_ref, xs_hbm, sbuf, zbuf, sem, zsem):
    i = pl.program_id(0)
    nb = pl.num_programs(0)
    rows = DISPATCH_TILE
    blk = DISPATCH_ROWS
    n_blocks = xs_hbm.shape[0] // blk

    def zero_copy(start):
        return pltpu.make_async_copy(zbuf, xs_hbm.at[pl.ds(pl.multiple_of(start, blk), blk)], zsem)

    @pl.when(i == 0)
    def _():
        zbuf[...] = jnp.zeros_like(zbuf)
        for e in range(N_EXPERTS):
            @pl.when(tail_ref[e] >= 0)
            def _():
                zero_copy(tail_ref[e]).start()

        def start_unused(b, carry):
            zero_copy(b * blk).start()
            return carry

        def wait_unused(b, carry):
            zero_copy(b * blk).wait()
            return carry

        lax.fori_loop(nvb_ref[0], n_blocks, start_unused, 0)
        for e in range(N_EXPERTS):
            @pl.when(tail_ref[e] >= 0)
            def _():
                zero_copy(tail_ref[e]).wait()
        lax.fori_loop(nvb_ref[0], n_blocks, wait_unused, 0)

    slot = i % 2

    def wait_slot(s):
        for _ in range(TOP_K):
            pltpu.make_async_copy(sbuf.at[s], xs_hbm.at[pl.ds(0, rows)], sem.at[s]).wait()

    @pl.when(i >= 2)
    def _():
        wait_slot(slot)

    sbuf[slot] = h_ref[...]

    def body(r, carry):
        for k in range(TOP_K):
            _row_scatter_copy(sbuf.at[slot], r, xs_hbm, pos_ref[(i * rows + r) * TOP_K + k],
                              sem.at[slot]).start()
        return carry

    lax.fori_loop(0, rows, body, 0, unroll=8)

    @pl.when(i == nb - 1)
    def _():
        @pl.when(nb >= 2)
        def _():
            wait_slot(1 - slot)
        wait_slot(slot)


def _dispatch(pos, tail, n_valid_blocks, h1, n_blocks):
    T, D = h1.shape
    rows = DISPATCH_TILE
    grid_spec = pltpu.PrefetchScalarGridSpec(
        num_scalar_prefetch=3,
        grid=(T // rows,),
        in_specs=[pl.BlockSpec((rows, D), lambda i, pos, tail, nvb: (i, 0))],
        out_specs=pl.BlockSpec(memory_space=pl.ANY),
        scratch_shapes=[pltpu.VMEM((2, rows, D), F32), pltpu.VMEM((DISPATCH_ROWS, D), F32),
                        pltpu.SemaphoreType.DMA((2,)), pltpu.SemaphoreType.DMA],
    )
    return pl.pallas_call(
        _dispatch_kernel,
        grid_spec=grid_spec,
        out_shape=jax.ShapeDtypeStruct((n_blocks * DISPATCH_ROWS, D), F32),
        compiler_params=_params(("arbitrary",)),
        name="dispatch",
    )(pos, tail, n_valid_blocks, h1)


def _expert_kernel(be_ref, nvb_ref, xs_ref, wg_ref, wu_ref, wd_ref, ys_ref, wgb, wub, wdb):
    i = pl.program_id(0)
    e = be_ref[i]
    prev = be_ref[jnp.maximum(i - 1, 0)]

    @pl.when(jnp.logical_or(i == 0, e != prev))
    def _():
        wgb[...] = wg_ref[...].astype(BF16)
        wub[...] = wu_ref[...].astype(BF16)
        wdb[...] = wd_ref[...].astype(BF16)

    @pl.when(i < nvb_ref[0])
    def _():
        x = xs_ref[...].astype(BF16)
        gate = jnp.dot(x, wgb[...], preferred_element_type=F32)
        up = jnp.dot(x, wub[...], preferred_element_type=F32)
        hid = (gate * jax.nn.sigmoid(gate) * up).astype(BF16)
        ys_ref[...] = jnp.dot(hid, wdb[...], preferred_element_type=F32)

    @pl.when(i >= nvb_ref[0])
    def _():
        ys_ref[...] = jnp.zeros_like(ys_ref)


def _experts(layer, block_e, n_valid_blocks, xs, w_gate, w_up, w_down):
    P, D = xs.shape
    DE = w_gate.shape[-1]
    rows = DISPATCH_ROWS
    n_blocks = P // rows
    x_map = lambda i, be, nvb: (jnp.minimum(i, jnp.maximum(nvb[0] - 1, 0)), 0)
    grid_spec = pltpu.PrefetchScalarGridSpec(
        num_scalar_prefetch=2,
        grid=(n_blocks,),
        in_specs=[pl.BlockSpec((rows, D), x_map),
                  pl.BlockSpec((None, None, D, DE), lambda i, be, nvb: (layer, be[i], 0, 0)),
                  pl.BlockSpec((None, None, D, DE), lambda i, be, nvb: (layer, be[i], 0, 0)),
                  pl.BlockSpec((None, None, DE, D), lambda i, be, nvb: (layer, be[i], 0, 0))],
        out_specs=pl.BlockSpec((rows, D), lambda i, be, nvb: (i, 0)),
        scratch_shapes=[pltpu.VMEM((D, DE), BF16), pltpu.VMEM((D, DE), BF16),
                        pltpu.VMEM((DE, D), BF16)],
    )
    return pl.pallas_call(
        _expert_kernel,
        grid_spec=grid_spec,
        out_shape=jax.ShapeDtypeStruct((P, D), F32),
        compiler_params=_params(("arbitrary",)),
        name="experts",
    )(block_e, n_valid_blocks, xs, w_gate, w_up, w_down)


def _combine_kernel(alpha, pos_ref, ys_hbm, h_ref, route_ref, g_ref, b_ref, o_ref, ybuf, sem):
    i = pl.program_id(0)
    nb = pl.num_programs(0)
    rows = COMBINE_ROWS

    def issue(blk, slot):
        def body(r, carry):
            for k in range(TOP_K):
                _row_gather_copy(ys_hbm, pos_ref[(blk * rows + r) * TOP_K + k],
                                 ybuf.at[slot, k], r, sem.at[slot]).start()
            return carry
        lax.fori_loop(0, rows, body, 0, unroll=8)

    @pl.when(i == 0)
    def _():
        issue(0, 0)

    @pl.when(i + 1 < nb)
    def _():
        issue(i + 1, (i + 1) % 2)

    slot = i % 2
    for k in range(TOP_K):
        pltpu.make_async_copy(ys_hbm.at[pl.ds(0, rows)], ybuf.at[slot, k], sem.at[slot]).wait()
    route = route_ref[...]
    moe = route[:, 2:3] * ybuf[slot, 0] + route[:, 3:4] * ybuf[slot, 1]
    o_ref[...] = _layer_norm(alpha * h_ref[...] + moe, g_ref[...], b_ref[...])


def _combine_norm(alpha, pos, ys, h1, route, g, b):
    T, D = h1.shape
    rows = COMBINE_ROWS
    grid_spec = pltpu.PrefetchScalarGridSpec(
        num_scalar_prefetch=1,
        grid=(T // rows,),
        in_specs=[pl.BlockSpec(memory_space=pl.ANY),
                  pl.BlockSpec((rows, D), lambda i, pos: (i, 0)),
                  pl.BlockSpec((rows, LANES), lambda i, pos: (i, 0)),
                  pl.BlockSpec((1, D), lambda i, pos: (0, 0)),
                  pl.BlockSpec((1, D), lambda i, pos: (0, 0))],
        out_specs=pl.BlockSpec((rows, D), lambda i, pos: (i, 0)),
        scratch_shapes=[pltpu.VMEM((2, TOP_K, rows, D), F32), pltpu.SemaphoreType.DMA((2,))],
    )
    return pl.pallas_call(
        functools.partial(_combine_kernel, alpha),
        grid_spec=grid_spec,
        out_shape=jax.ShapeDtypeStruct((T, D), F32),
        compiler_params=_params(("arbitrary",)),
        name="combine_norm",
    )(pos, ys, h1, route, g, b)


def _dispatch_plan(route, counts_row, n_blocks):
    rows = DISPATCH_ROWS
    experts = jnp.arange(N_EXPERTS, dtype=jnp.int32)
    counts = counts_row[0, N_EXPERT_GROUPS:N_EXPERT_GROUPS + N_EXPERTS].astype(jnp.int32)
    padded = ((counts + rows - 1) // rows) * rows
    pend = jnp.cumsum(padded)
    pstart = pend - padded
    block_start = jnp.arange(n_blocks, dtype=jnp.int32) * rows
    block_e = jnp.minimum(jnp.sum((block_start[:, None] >= pend[None, :]).astype(jnp.int32), axis=1),
                          N_EXPERTS - 1).astype(jnp.int32)
    n_valid_blocks = (pend[-1:] // rows).astype(jnp.int32)
    tail = jnp.where(counts > 0, pend - rows, -1).astype(jnp.int32)
    e = route[:, :TOP_K].astype(jnp.int32)
    rank = route[:, 4:4 + TOP_K].astype(jnp.int32)
    base = jnp.sum(jnp.where(e[:, :, None] == experts[None, None, :], pstart[None, None, :], 0),
                   axis=-1)
    pos = (base + rank).reshape(-1).astype(jnp.int32)
    return block_e, n_valid_blocks, tail, pos


def kernel(x, w_in, w_pool, pool_scale, w_out, ln1_g, ln1_b, w_router_group, b_router_group,
           w_router_expert, b_router_expert, w_gate, w_up, w_down, ln2_g, ln2_b):
    B, S, D = x.shape
    depth = w_in.shape[0]
    T = B * S
    alpha = (2.0 * depth) ** 0.25
    pool_width = w_pool.shape[1] * w_pool.shape[2]
    sb_width = (w_in.shape[2] - pool_width) // 3
    n_blocks = -(-T * TOP_K // DISPATCH_ROWS) + N_EXPERTS

    col_scale = jnp.concatenate([jnp.ones((pool_width,), F32),
                                 jnp.full((sb_width,), SB_HEAD_DIM ** -0.5, F32),
                                 jnp.ones((2 * sb_width,), F32)])
    w_in_s = w_in * col_scale
    w_u = w_in_s[:, :, :pool_width].astype(BF16)
    w_qt = jnp.swapaxes(w_in_s[:, :, pool_width:pool_width + sb_width], 1, 2).astype(BF16)
    w_k = w_in_s[:, :, pool_width + sb_width:pool_width + 2 * sb_width].astype(BF16)
    w_vt = jnp.swapaxes(w_in_s[:, :, pool_width + 2 * sb_width:], 1, 2).astype(BF16)
    w_pool_b = w_pool.astype(BF16)
    w_out_b = w_out.astype(BF16)
    w_r = jnp.concatenate([w_router_group, w_router_expert,
                           jnp.zeros((depth, D, LANES - N_EXPERT_GROUPS - N_EXPERTS), F32)], axis=-1)
    wr_hi = w_r.astype(BF16)
    wr_lo = (w_r - wr_hi.astype(F32)).astype(BF16)
    b_r = jnp.concatenate([b_router_group.astype(F32),
                           b_router_expert.astype(F32).reshape(depth, N_EXPERTS),
                           jnp.zeros((depth, LANES - N_EXPERT_GROUPS - N_EXPERTS), F32)], axis=-1)

    h = x.reshape(T, D)
    for l in range(depth):
        u, q_t, k_perm, v_t = _project(h, w_u[l], w_qt[l], w_k[l], w_vt[l])
        y_pool = _pool(u, w_pool_b[l], pool_scale[l][None, :], B, S)
        y_sb = _attention(q_t, k_perm, v_t, B, S)
        h1, route, counts = _mix_norm_route(alpha, y_pool, y_sb, h, w_out_b[l], ln1_g[l][None, :],
                                            ln1_b[l][None, :], wr_hi[l], wr_lo[l], b_r[l][None, :])
        block_e, n_valid_blocks, tail, pos = _dispatch_plan(route, counts, n_blocks)
        xs = _dispatch(pos, tail, n_valid_blocks, h1, n_blocks)
        ys = _experts(l, block_e, n_valid_blocks, xs, w_gate, w_up, w_down)
        h = _combine_norm(alpha, pos, ys, h1, route, ln2_g[l][None, :], ln2_b[l][None, :])
    return h.reshape(B, S, D)
```

```python
import functools

import jax
import jax.numpy as jnp
from jax import lax
from jax.experimental import pallas as pl
from jax.experimental.pallas import tpu as pltpu

F32 = jnp.float32
BF16 = jnp.bfloat16

POOL_WINDOWS = (2, 4, 8, 16)
SB_HEAD_DIM = 64
N_EXPERT_GROUPS = 4
EXPERTS_PER_GROUP = 8
N_EXPERTS = N_EXPERT_GROUPS * EXPERTS_PER_GROUP
TOP_K = 2
LN_EPS = 1e-5

LANES = 128
SUBLANES = 8
ROW_TILE = 512
QUERY_BLOCK = 256
KEY_BLOCK = 128
KEY_SEG = KEY_BLOCK // SUBLANES
PAIRS_PER_STEP = 2
DISPATCH_ROWS = 256
DISPATCH_TILE = 256
COMBINE_ROWS = 256
NEG_BIG = -1e30
F32_SIGN_BIT = 0x80000000
VMEM_LIMIT = 48 * 1024 * 1024


def _params(sem):
    return pltpu.CompilerParams(dimension_semantics=sem, vmem_limit_bytes=VMEM_LIMIT)


def _proj_kernel(x_ref, wu_ref, wqt_ref, wk_ref, wvt_ref, u_ref, qt_ref, k_ref, vt_ref):
    rows = x_ref.shape[0]
    nt = (((1,), (1,)), ((), ()))
    xb = x_ref[...].astype(BF16)
    u_ref[...] = jnp.dot(xb, wu_ref[...], preferred_element_type=F32)
    qt_ref[...] = lax.dot_general(wqt_ref[...], xb, nt,
                                  preferred_element_type=F32).astype(qt_ref.dtype)
    p_i = lax.broadcasted_iota(jnp.int32, (KEY_BLOCK, KEY_BLOCK), 0)
    t_i = lax.broadcasted_iota(jnp.int32, (KEY_BLOCK, KEY_BLOCK), 1)
    src = (p_i % SUBLANES) * KEY_SEG + p_i // SUBLANES
    perm = jnp.where(t_i == src, 1.0, 0.0).astype(BF16)
    xp = jnp.concatenate(
        [jnp.dot(perm, xb[g * KEY_BLOCK:(g + 1) * KEY_BLOCK, :], preferred_element_type=F32)
         for g in range(rows // KEY_BLOCK)], axis=0).astype(BF16)
    k_ref[...] = jnp.dot(xp, wk_ref[...], preferred_element_type=F32).astype(k_ref.dtype)
    vt_ref[...] = lax.dot_general(wvt_ref[...], xp, nt,
                                  preferred_element_type=F32).astype(vt_ref.dtype)


def _project(h2d, w_u, w_qt, w_k, w_vt):
    T, D = h2d.shape
    width = w_u.shape[1]
    row_major = lambda: pl.BlockSpec((ROW_TILE, width), lambda i: (i, 0))
    feat_major = lambda: pl.BlockSpec((width, ROW_TILE), lambda i: (0, i))
    full = lambda a: pl.BlockSpec(a.shape, lambda i: (0, 0))
    return pl.pallas_call(
        _proj_kernel,
        grid=(T // ROW_TILE,),
        in_specs=[pl.BlockSpec((ROW_TILE, D), lambda i: (i, 0)),
                  full(w_u), full(w_qt), full(w_k), full(w_vt)],
        out_specs=[row_major(), feat_major(), row_major(), feat_major()],
        out_shape=[jax.ShapeDtypeStruct((T, width), F32),
                   jax.ShapeDtypeStruct((width, T), BF16),
                   jax.ShapeDtypeStruct((T, width), BF16),
                   jax.ShapeDtypeStruct((width, T), BF16)],
        compiler_params=_params(("parallel",)),
        name="proj",
    )(h2d, w_u, w_qt, w_k, w_vt)


def _pool_kernel(u_ref, w_ref, s_ref, o_ref):
    S = u_ref.shape[0]
    row = lax.broadcasted_iota(jnp.int32, (S, LANES), 0)
    for g, win in enumerate(POOL_WINDOWS):
        u = u_ref[:, g * LANES:(g + 1) * LANES]
        acc = u
        k = 1
        while k < win:
            shifted = jnp.where(row >= k, pltpu.roll(acc, k, 0), 0.0)
            acc = acc + shifted
            k *= 2
        count = jnp.minimum(row + 1, win).astype(F32)
        pooled = acc / count - u
        mixed = jnp.dot(pooled.astype(BF16), w_ref[g], preferred_element_type=F32)
        o_ref[:, g * LANES:(g + 1) * LANES] = (
            mixed * s_ref[:, g * LANES:(g + 1) * LANES]).astype(o_ref.dtype)


def _pool(u, w_pool_b, pool_scale, B, S):
    T, W = u.shape
    return pl.pallas_call(
        _pool_kernel,
        grid=(B,),
        in_specs=[pl.BlockSpec((S, W), lambda b: (b, 0)),
                  pl.BlockSpec(w_pool_b.shape, lambda b: (0, 0, 0)),
                  pl.BlockSpec((1, W), lambda b: (0, 0))],
        out_specs=pl.BlockSpec((S, W), lambda b: (b, 0)),
        out_shape=jax.ShapeDtypeStruct((T, W), BF16),
        compiler_params=_params(("parallel",)),
        name="pool",
    )(u, w_pool_b, pool_scale)


def _attn_kernel(qt_ref, k_ref, vt_ref, o_ref, acc_ref, car_ref, z0_ref, z1_ref, a0_ref, a1_ref):
    i = pl.program_id(2)
    qb = QUERY_BLOCK
    kb = KEY_BLOCK
    width = 2 * qb
    pairs = range(PAIRS_PER_STEP)
    pair_rows = [slice(p * LANES, (p + 1) * LANES) for p in pairs]
    feat = lax.broadcasted_iota(jnp.int32, (LANES, qb), 0)
    rhs = []
    for p in pairs:
        q2 = qt_ref[pair_rows[p], :]
        zero = jnp.zeros_like(q2)
        rhs.append(jnp.concatenate([jnp.where(feat < SB_HEAD_DIM, q2, zero),
                                    jnp.where(feat >= SB_HEAD_DIM, q2, zero)], axis=1))
    sub = lax.broadcasted_iota(jnp.int32, (SUBLANES, width), 0)
    lane = lax.broadcasted_iota(jnp.int32, (SUBLANES, width), 1)
    q_pos = i * qb + jnp.bitwise_and(lane, qb - 1)

    acc_ref[...] = jnp.zeros_like(acc_ref)
    car_ref[...] = jnp.zeros_like(car_ref)

    n_diag = qb // kb
    last = (i + 1) * n_diag - 1

    def block_start(n):
        return pl.multiple_of(jnp.clip(last - n, 0, last) * kb, kb)

    def scores(n, z_ref):
        for p in pairs:
            z_ref[p] = jnp.dot(k_ref[pl.ds(block_start(n), kb), pair_rows[p]], rhs[p],
                               preferred_element_type=F32)

    def values(n, a_ref):
        for p in pairs:
            acc_ref[p] += jnp.dot(vt_ref[pair_rows[p], pl.ds(block_start(n), kb)], a_ref[p],
                                  preferred_element_type=F32)

    def weights(n, z_ref, a_ref, masked):
        for p in pairs:
            weights_one(n, z_ref.at[p], a_ref.at[p], car_ref.at[p], masked)

    def weights_one(n, z_ref, a_ref, car_ref, masked):
        start = block_start(n)
        zs, sps, sufs, masks = [], [], [], []
        run = jnp.zeros((SUBLANES, width), F32)
        for v in reversed(range(KEY_SEG)):
            zv = z_ref[v * SUBLANES:(v + 1) * SUBLANES, :]
            neg_abs = lax.bitcast_convert_type(
                lax.bitcast_convert_type(zv, jnp.uint32) | jnp.uint32(F32_SIGN_BIT), F32)
            sp = jnp.maximum(zv, 0.0) + jnp.log(1.0 + jnp.exp(neg_abs))
            if masked:
                m = (start + sub * KEY_SEG + v) < q_pos
                sp = jnp.where(m, sp, 0.0)
                masks.append(m)
            zs.append(zv)
            sps.append(sp)
            sufs.append(run)
            run = run + sp
        incl = run
        for step in (1, 2, 4):
            shifted = pltpu.roll(incl, SUBLANES - step, 0)
            incl = incl + jnp.where(sub < SUBLANES - step, shifted, 0.0)
        base = (incl - run) + car_ref[...]
        parts = []
        for idx in range(KEY_SEG):
            a = jnp.exp(zs[idx] - sps[idx] - (sufs[idx] + base))
            if masked:
                a = jnp.where(masks[idx], a, 0.0)
            parts.append(a)
        a_ref[...] = jnp.concatenate(parts[::-1], axis=0).astype(BF16)
        car_ref[...] += jnp.broadcast_to(incl[0:1, :], (SUBLANES, width))

    assert n_diag == 2
    scores(0, z0_ref)
    scores(1, z1_ref)
    weights(0, z0_ref, a0_ref, True)
    scores(2, z0_ref)
    weights(1, z1_ref, a1_ref, True)
    values(0, a0_ref)

    def pair(t, carry):
        n = 2 + 2 * t
        values(n - 1, a1_ref)
        weights(n, z0_ref, a0_ref, False)
        scores(n + 1, z1_ref)
        values(n, a0_ref)
        weights(n + 1, z1_ref, a1_ref, False)
        scores(n + 2, z0_ref)
        return carry

    lax.fori_loop(0, i, pair, 0)
    values(last, a1_ref)

    for p in pairs:
        acc = acc_ref[p]
        both = jnp.where(feat < SB_HEAD_DIM, acc[:, :qb], acc[:, qb:])
        o_ref[:, pair_rows[p]] = both.T.astype(o_ref.dtype)


def _attention(q_t, k_perm, v_t, B, S):
    W, T = q_t.shape
    qb = QUERY_BLOCK
    nq = S // qb
    step_w = PAIRS_PER_STEP * LANES
    groups = W // step_w
    per_pair = lambda shape, dtype: pltpu.VMEM((PAIRS_PER_STEP,) + shape, dtype)
    return pl.pallas_call(
        _attn_kernel,
        grid=(B, groups, nq),
        in_specs=[pl.BlockSpec((step_w, qb), lambda b, g, i: (g, b * nq + i)),
                  pl.BlockSpec((S, step_w), lambda b, g, i: (b, g)),
                  pl.BlockSpec((step_w, S), lambda b, g, i: (g, b))],
        out_specs=pl.BlockSpec((qb, step_w), lambda b, g, i: (b * nq + i, g)),
        out_shape=jax.ShapeDtypeStruct((T, W), BF16),
        scratch_shapes=[per_pair((LANES, 2 * qb), F32), per_pair((SUBLANES, 2 * qb), F32),
                        per_pair((KEY_BLOCK, 2 * qb), F32), per_pair((KEY_BLOCK, 2 * qb), F32),
                        per_pair((KEY_BLOCK, 2 * qb), BF16), per_pair((KEY_BLOCK, 2 * qb), BF16)],
        compiler_params=_params(("parallel", "parallel", "parallel")),
        name="attn",
    )(q_t, k_perm, v_t)


def _layer_norm(y, g, b):
    mu = jnp.mean(y, axis=-1, keepdims=True)
    yc = y - mu
    var = jnp.mean(yc * yc, axis=-1, keepdims=True)
    return yc * lax.rsqrt(var + LN_EPS) * g + b


def _mix_kernel(alpha, yp_ref, ys_ref, h_ref, wo_ref, g_ref, b_ref, wrh_ref, wrl_ref, br_ref,
                h1_ref, route_ref, cnt_ref, run_ref):
    @pl.when(pl.program_id(0) == 0)
    def _():
        run_ref[...] = jnp.zeros_like(run_ref)

    half = yp_ref.shape[1]
    mix = (jnp.dot(yp_ref[...], wo_ref[:half, :], preferred_element_type=F32) +
           jnp.dot(ys_ref[...], wo_ref[half:, :], preferred_element_type=F32))
    h1 = _layer_norm(alpha * h_ref[...] + mix, g_ref[...], b_ref[...])
    h1_ref[...] = h1

    hh = h1.astype(BF16)
    hl = (h1 - hh.astype(F32)).astype(BF16)
    logits = (jnp.dot(hh, wrh_ref[...], preferred_element_type=F32) +
              jnp.dot(hl, wrh_ref[...], preferred_element_type=F32) +
              jnp.dot(hh, wrl_ref[...], preferred_element_type=F32) + br_ref[...])
    col = lax.broadcasted_iota(jnp.int32, logits.shape, 1)
    gl = jnp.where(col < N_EXPERT_GROUPS, logits, NEG_BIG)
    gmax = jnp.max(gl, axis=-1, keepdims=True)
    gidx = jnp.min(jnp.where(gl == gmax, col, LANES), axis=-1, keepdims=True)
    g_p = 1.0 / jnp.sum(jnp.exp(gl - gmax), axis=-1, keepdims=True)
    lo_col = N_EXPERT_GROUPS + EXPERTS_PER_GROUP * gidx
    el = jnp.where(col >= lo_col, jnp.where(col < lo_col + EXPERTS_PER_GROUP, logits, NEG_BIG),
                   NEG_BIG)
    m1 = jnp.max(el, axis=-1, keepdims=True)
    i1 = jnp.min(jnp.where(el == m1, col, LANES), axis=-1, keepdims=True)
    el2 = jnp.where(col == i1, NEG_BIG, el)
    m2 = jnp.max(el2, axis=-1, keepdims=True)
    i2 = jnp.min(jnp.where(el2 == m2, col, LANES), axis=-1, keepdims=True)
    ratio = jnp.exp(m2 - m1)
    gate1 = g_p / (1.0 + ratio)
    gate2 = g_p * ratio / (1.0 + ratio)
    e1 = (i1 - N_EXPERT_GROUPS).astype(F32)
    e2 = (i2 - N_EXPERT_GROUPS).astype(F32)

    rows = logits.shape[0]
    hit1 = col == i1
    hit2 = col == i2
    onehot = jnp.where(hit1, 1.0, jnp.where(hit2, 1.0, 0.0))
    r_i = lax.broadcasted_iota(jnp.int32, (rows, rows), 0)
    c_i = lax.broadcasted_iota(jnp.int32, (rows, rows), 1)
    earlier = jnp.where(c_i < r_i, 1.0, 0.0).astype(BF16)
    before = run_ref[...] + jnp.dot(earlier, onehot.astype(BF16), preferred_element_type=F32)
    rank1 = jnp.sum(jnp.where(hit1, before, 0.0), axis=-1, keepdims=True)
    rank2 = jnp.sum(jnp.where(hit2, before, 0.0), axis=-1, keepdims=True)
    run_ref[...] += jnp.sum(onehot, axis=0, keepdims=True)
    cnt_ref[...] = run_ref[...]

    route_ref[...] = jnp.where(col == 0, e1, jnp.where(col == 1, e2,
                               jnp.where(col == 2, gate1, jnp.where(col == 3, gate2,
                               jnp.where(col == 4, rank1, jnp.where(col == 5, rank2, 0.0))))))


def _mix_norm_route(alpha, y_pool, y_sb, h2d, w_out_b, g, b, wr_hi, wr_lo, b_r):
    T, D = h2d.shape
    half = y_pool.shape[1]
    full = lambda a: pl.BlockSpec(a.shape, lambda i: (0,) * a.ndim)
    return pl.pallas_call(
        functools.partial(_mix_kernel, alpha),
        grid=(T // ROW_TILE,),
        in_specs=[pl.BlockSpec((ROW_TILE, half), lambda i: (i, 0)),
                  pl.BlockSpec((ROW_TILE, half), lambda i: (i, 0)),
                  pl.BlockSpec((ROW_TILE, D), lambda i: (i, 0)),
                  full(w_out_b), full(g), full(b), full(wr_hi), full(wr_lo), full(b_r)],
        out_specs=[pl.BlockSpec((ROW_TILE, D), lambda i: (i, 0)),
                   pl.BlockSpec((ROW_TILE, LANES), lambda i: (i, 0)),
                   pl.BlockSpec((1, LANES), lambda i: (0, 0))],
        out_shape=[jax.ShapeDtypeStruct((T, D), F32), jax.ShapeDtypeStruct((T, LANES), F32),
                   jax.ShapeDtypeStruct((1, LANES), F32)],
        scratch_shapes=[pltpu.VMEM((1, LANES), F32)],
        compiler_params=_params(("arbitrary",)),
        name="mix_norm_route",
    )(y_pool, y_sb, h2d, w_out_b, g, b, wr_hi, wr_lo, b_r)


def _row_gather_copy(src_hbm, row, dst_ref, dst_row, sem):
    return pltpu.make_async_copy(src_hbm.at[pl.ds(row, 1)], dst_ref.at[pl.ds(dst_row, 1)], sem)


def _row_scatter_copy(src_ref, src_row, dst_hbm, row, sem):
    return pltpu.make_async_copy(src_ref.at[pl.ds(src_row, 1)], dst_hbm.at[pl.ds(row, 1)], sem)


def _dispatch_kernel(pos_ref, tail_ref, nvb_ref, h_ref, xs_hbm, sbuf, zbuf, sem, zsem):
    i = pl.program_id(0)
    nb = pl.num_programs(0)
    rows = DISPATCH_TILE
    blk = DISPATCH_ROWS
    n_blocks = xs_hbm.shape[0] // blk

    def zero_copy(start):
        return pltpu.make_async_copy(zbuf, xs_hbm.at[pl.ds(pl.multiple_of(start, blk), blk)], zsem)

    @pl.when(i == 0)
    def _():
        zbuf[...] = jnp.zeros_like(zbuf)
        for e in range(N_EXPERTS):
            @pl.when(tail_ref[e] >= 0)
            def _():
                zero_copy(tail_ref[e]).start()

        def start_unused(b, carry):
            zero_copy(b * blk).start()
            return carry

        def wait_unused(b, carry):
            zero_copy(b * blk).wait()
            return carry

        lax.fori_loop(nvb_ref[0], n_blocks, start_unused, 0)
        for e in range(N_EXPERTS):
            @pl.when(tail_ref[e] >= 0)
            def _():
                zero_copy(tail_ref[e]).wait()
        lax.fori_loop(nvb_ref[0], n_blocks, wait_unused, 0)

    slot = i % 2

    def wait_slot(s):
        for _ in range(TOP_K):
            pltpu.make_async_copy(sbuf.at[s], xs_hbm.at[pl.ds(0, rows)], sem.at[s]).wait()

    @pl.when(i >= 2)
    def _():
        wait_slot(slot)

    sbuf[slot] = h_ref[...]

    def body(r, carry):
        for k in range(TOP_K):
            _row_scatter_copy(sbuf.at[slot], r, xs_hbm, pos_ref[(i * rows + r) * TOP_K + k],
                              sem.at[slot]).start()
        return carry

    lax.fori_loop(0, rows, body, 0, unroll=8)

    @pl.when(i == nb - 1)
    def _():
        @pl.when(nb >= 2)
        def _():
            wait_slot(1 - slot)
        wait_slot(slot)


def _dispatch(pos, tail, n_valid_blocks, h1, n_blocks):
    T, D = h1.shape
    rows = DISPATCH_TILE
    grid_spec = pltpu.PrefetchScalarGridSpec(
        num_scalar_prefetch=3,
        grid=(T // rows,),
        in_specs=[pl.BlockSpec((rows, D), lambda i, pos, tail, nvb: (i, 0))],
        out_specs=pl.BlockSpec(memory_space=pl.ANY),
        scratch_shapes=[pltpu.VMEM((2, rows, D), F32), pltpu.VMEM((DISPATCH_ROWS, D), F32),
                        pltpu.SemaphoreType.DMA((2,)), pltpu.SemaphoreType.DMA],
    )
    return pl.pallas_call(
        _dispatch_kernel,
        grid_spec=grid_spec,
        out_shape=jax.ShapeDtypeStruct((n_blocks * DISPATCH_ROWS, D), F32),
        compiler_params=_params(("arbitrary",)),
        name="dispatch",
    )(pos, tail, n_valid_blocks, h1)


def _expert_kernel(be_ref, nvb_ref, xs_ref, wg_ref, wu_ref, wd_ref, ys_ref, wgb, wub, wdb):
    i = pl.program_id(0)
    e = be_ref[i]
    prev = be_ref[jnp.maximum(i - 1, 0)]

    @pl.when(jnp.logical_or(i == 0, e != prev))
    def _():
        wgb[...] = wg_ref[...].astype(BF16)
        wub[...] = wu_ref[...].astype(BF16)
        wdb[...] = wd_ref[...].astype(BF16)

    @pl.when(i < nvb_ref[0])
    def _():
        x = xs_ref[...].astype(BF16)
        gate = jnp.dot(x, wgb[...], preferred_element_type=F32)
        up = jnp.dot(x, wub[...], preferred_element_type=F32)
        hid = (gate * jax.nn.sigmoid(gate) * up).astype(BF16)
        ys_ref[...] = jnp.dot(hid, wdb[...], preferred_element_type=F32)

    @pl.when(i >= nvb_ref[0])
    def _():
        ys_ref[...] = jnp.zeros_like(ys_ref)


def _experts(layer, block_e, n_valid_blocks, xs, w_gate, w_up, w_down):
    P, D = xs.shape
    DE = w_gate.shape[-1]
    rows = DISPATCH_ROWS
    n_blocks = P // rows
    x_map = lambda i, be, nvb: (jnp.minimum(i, jnp.maximum(nvb[0] - 1, 0)), 0)
    grid_spec = pltpu.PrefetchScalarGridSpec(
        num_scalar_prefetch=2,
        grid=(n_blocks,),
        in_specs=[pl.BlockSpec((rows, D), x_map),
                  pl.BlockSpec((None, None, D, DE), lambda i, be, nvb: (layer, be[i], 0, 0)),
                  pl.BlockSpec((None, None, D, DE), lambda i, be, nvb: (layer, be[i], 0, 0)),
                  pl.BlockSpec((None, None, DE, D), lambda i, be, nvb: (layer, be[i], 0, 0))],
        out_specs=pl.BlockSpec((rows, D), lambda i, be, nvb: (i, 0)),
        scratch_shapes=[pltpu.VMEM((D, DE), BF16), pltpu.VMEM((D, DE), BF16),
                        pltpu.VMEM((DE, D), BF16)],
    )
    return pl.pallas_call(
        _expert_kernel,
        grid_spec=grid_spec,
        out_shape=jax.ShapeDtypeStruct((P, D), F32),
        compiler_params=_params(("arbitrary",)),
        name="experts",
    )(block_e, n_valid_blocks, xs, w_gate, w_up, w_down)


def _combine_kernel(alpha, pos_ref, ys_hbm, h_ref, route_ref, g_ref, b_ref, o_ref, ybuf, sem):
    i = pl.program_id(0)
    nb = pl.num_programs(0)
    rows = COMBINE_ROWS

    def issue(blk, slot):
        def body(r, carry):
            for k in range(TOP_K):
                _row_gather_copy(ys_hbm, pos_ref[(blk * rows + r) * TOP_K + k],
                                 ybuf.at[slot, k], r, sem.at[slot]).start()
            return carry
        lax.fori_loop(0, rows, body, 0, unroll=8)

    @pl.when(i == 0)
    def _():
        issue(0, 0)

    @pl.when(i + 1 < nb)
    def _():
        issue(i + 1, (i + 1) % 2)

    slot = i % 2
    for k in range(TOP_K):
        pltpu.make_async_copy(ys_hbm.at[pl.ds(0, rows)], ybuf.at[slot, k], sem.at[slot]).wait()
    route = route_ref[...]
    moe = route[:, 2:3] * ybuf[slot, 0] + route[:, 3:4] * ybuf[slot, 1]
    o_ref[...] = _layer_norm(alpha * h_ref[...] + moe, g_ref[...], b_ref[...])


def _combine_norm(alpha, pos, ys, h1, route, g, b):
    T, D = h1.shape
    rows = COMBINE_ROWS
    grid_spec = pltpu.PrefetchScalarGridSpec(
        num_scalar_prefetch=1,
        grid=(T // rows,),
        in_specs=[pl.BlockSpec(memory_space=pl.ANY),
                  pl.BlockSpec((rows, D), lambda i, pos: (i, 0)),
                  pl.BlockSpec((rows, LANES), lambda i, pos: (i, 0)),
                  pl.BlockSpec((1, D), lambda i, pos: (0, 0)),
                  pl.BlockSpec((1, D), lambda i, pos: (0, 0))],
        out_specs=pl.BlockSpec((rows, D), lambda i, pos: (i, 0)),
        scratch_shapes=[pltpu.VMEM((2, TOP_K, rows, D), F32), pltpu.SemaphoreType.DMA((2,))],
    )
    return pl.pallas_call(
        functools.partial(_combine_kernel, alpha),
        grid_spec=grid_spec,
        out_shape=jax.ShapeDtypeStruct((T, D), F32),
        compiler_params=_params(("arbitrary",)),
        name="combine_norm",
    )(pos, ys, h1, route, g, b)


def _dispatch_plan(route, counts_row, n_blocks):
    rows = DISPATCH_ROWS
    experts = jnp.arange(N_EXPERTS, dtype=jnp.int32)
    counts = counts_row[0, N_EXPERT_GROUPS:N_EXPERT_GROUPS + N_EXPERTS].astype(jnp.int32)
    padded = ((counts + rows - 1) // rows) * rows
    pend = jnp.cumsum(padded)
    pstart = pend - padded
    block_start = jnp.arange(n_blocks, dtype=jnp.int32) * rows
    block_e = jnp.minimum(jnp.sum((block_start[:, None] >= pend[None, :]).astype(jnp.int32), axis=1),
                          N_EXPERTS - 1).astype(jnp.int32)
    n_valid_blocks = (pend[-1:] // rows).astype(jnp.int32)
    tail = jnp.where(counts > 0, pend - rows, -1).astype(jnp.int32)
    e = route[:, :TOP_K].astype(jnp.int32)
    rank = route[:, 4:4 + TOP_K].astype(jnp.int32)
    base = jnp.sum(jnp.where(e[:, :, None] == experts[None, None, :], pstart[None, None, :], 0),
                   axis=-1)
    pos = (base + rank).reshape(-1).astype(jnp.int32)
    return block_e, n_valid_blocks, tail, pos


def kernel(x, w_in, w_pool, pool_scale, w_out, ln1_g, ln1_b, w_router_group, b_router_group,
           w_router_expert, b_router_expert, w_gate, w_up, w_down, ln2_g, ln2_b):
    B, S, D = x.shape
    depth = w_in.shape[0]
    T = B * S
    alpha = (2.0 * depth) ** 0.25
    pool_width = w_pool.shape[1] * w_pool.shape[2]
    sb_width = (w_in.shape[2] - pool_width) // 3
    n_blocks = -(-T * TOP_K // DISPATCH_ROWS) + N_EXPERTS

    col_scale = jnp.concatenate([jnp.ones((pool_width,), F32),
                                 jnp.full((sb_width,), SB_HEAD_DIM ** -0.5, F32),
                                 jnp.ones((2 * sb_width,), F32)])
    w_in_s = w_in * col_scale
    w_u = w_in_s[:, :, :pool_width].astype(BF16)
    w_qt = jnp.swapaxes(w_in_s[:, :, pool_width:pool_width + sb_width], 1, 2).astype(BF16)
    w_k = w_in_s[:, :, pool_width + sb_width:pool_width + 2 * sb_width].astype(BF16)
    w_vt = jnp.swapaxes(w_in_s[:, :, pool_width + 2 * sb_width:], 1, 2).astype(BF16)
    w_pool_b = w_pool.astype(BF16)
    w_out_b = w_out.astype(BF16)
    w_r = jnp.concatenate([w_router_group, w_router_expert,
                           jnp.zeros((depth, D, LANES - N_EXPERT_GROUPS - N_EXPERTS), F32)], axis=-1)
    wr_hi = w_r.astype(BF16)
    wr_lo = (w_r - wr_hi.astype(F32)).astype(BF16)
    b_r = jnp.concatenate([b_router_group.astype(F32),
                           b_router_expert.astype(F32).reshape(depth, N_EXPERTS),
                           jnp.zeros((depth, LANES - N_EXPERT_GROUPS - N_EXPERTS), F32)], axis=-1)

    h = x.reshape(T, D)
    for l in range(depth):
        u, q_t, k_perm, v_t = _project(h, w_u[l], w_qt[l], w_k[l], w_vt[l])
        y_pool = _pool(u, w_pool_b[l], pool_scale[l][None, :], B, S)
        y_sb = _attention(q_t, k_perm, v_t, B, S)
        h1, route, counts = _mix_norm_route(alpha, y_pool, y_sb, h, w_out_b[l], ln1_g[l][None, :],
                                            ln1_b[l][None, :], wr_hi[l], wr_lo[l], b_r[l][None, :])
        block_e, n_valid_blocks, tail, pos = _dispatch_plan(route, counts, n_blocks)
        xs = _dispatch(pos, tail, n_valid_blocks, h1, n_blocks)
        ys = _experts(l, block_e, n_valid_blocks, xs, w_gate, w_up, w_down)
        h = _combine_norm(alpha, pos, ys, h1, route, ln2_g[l][None, :], ln2_b[l][None, :])
    return h.reshape(B, S, D)
```

```python
import functools

import jax
import jax.numpy as jnp
from jax import lax
from jax.experimental import pallas as pl
from jax.experimental.pallas import tpu as pltpu

F32 = jnp.float32
BF16 = jnp.bfloat16

POOL_WINDOWS = (2, 4, 8, 16)
SB_HEAD_DIM = 64
N_EXPERT_GROUPS = 4
EXPERTS_PER_GROUP = 8
N_EXPERTS = N_EXPERT_GROUPS * EXPERTS_PER_GROUP
TOP_K = 2
LN_EPS = 1e-5

LANES = 128
SUBLANES = 8
ROW_TILE = 512
QUERY_BLOCK = 256
KEY_BLOCK = 128
KEY_SEG = KEY_BLOCK // SUBLANES
PAIRS_PER_STEP = 2
DISPATCH_ROWS = 256
DISPATCH_TILE = 256
NEG_BIG = -1e30
F32_SIGN_BIT = 0x80000000
VMEM_LIMIT = 48 * 1024 * 1024


def _params(sem):
    return pltpu.CompilerParams(dimension_semantics=sem, vmem_limit_bytes=VMEM_LIMIT)


def _proj_kernel(x_ref, wu_ref, wqt_ref, wk_ref, wvt_ref, u_ref, qt_ref, k_ref, vt_ref,
                 xb_ref, xp_ref):
    for stage in _projection_stages(x_ref, (wu_ref, wqt_ref, wk_ref, wvt_ref),
                                    (u_ref, qt_ref, k_ref, vt_ref), xb_ref, xp_ref):
        stage()


def _projection_stages(x_ref, w_refs, out_refs, xb_ref, xp_ref):
    wu_ref, wqt_ref, wk_ref, wvt_ref = w_refs
    u_ref, qt_ref, k_ref, vt_ref = out_refs
    rows = x_ref.shape[0]
    nt = (((1,), (1,)), ((), ()))

    def pool_input():
        xb_ref[...] = x_ref[...].astype(BF16)
        u_ref[...] = jnp.dot(xb_ref[...], wu_ref[...], preferred_element_type=F32)

    def queries():
        qt_ref[...] = lax.dot_general(wqt_ref[...], xb_ref[...], nt,
                                      preferred_element_type=F32).astype(qt_ref.dtype)

    def keys():
        p_i = lax.broadcasted_iota(jnp.int32, (KEY_BLOCK, KEY_BLOCK), 0)
        t_i = lax.broadcasted_iota(jnp.int32, (KEY_BLOCK, KEY_BLOCK), 1)
        src = (p_i % SUBLANES) * KEY_SEG + p_i // SUBLANES
        perm = jnp.where(t_i == src, 1.0, 0.0).astype(BF16)
        for g in range(rows // KEY_BLOCK):
            blk = slice(g * KEY_BLOCK, (g + 1) * KEY_BLOCK)
            xp_ref[blk, :] = jnp.dot(perm, xb_ref[blk, :],
                                     preferred_element_type=F32).astype(BF16)
        k_ref[...] = jnp.dot(xp_ref[...], wk_ref[...],
                             preferred_element_type=F32).astype(k_ref.dtype)

    def values():
        vt_ref[...] = lax.dot_general(wvt_ref[...], xp_ref[...], nt,
                                      preferred_element_type=F32).astype(vt_ref.dtype)

    return pool_input, queries, keys, values


def _project(h2d, w_u, w_qt, w_k, w_vt):
    T, D = h2d.shape
    width = w_u.shape[1]
    row_major = lambda: pl.BlockSpec((ROW_TILE, width), lambda i: (i, 0))
    feat_major = lambda: pl.BlockSpec((width, ROW_TILE), lambda i: (0, i))
    full = lambda a: pl.BlockSpec(a.shape, lambda i: (0, 0))
    return pl.pallas_call(
        _proj_kernel,
        grid=(T // ROW_TILE,),
        in_specs=[pl.BlockSpec((ROW_TILE, D), lambda i: (i, 0)),
                  full(w_u), full(w_qt), full(w_k), full(w_vt)],
        out_specs=[row_major(), feat_major(), row_major(), feat_major()],
        out_shape=[jax.ShapeDtypeStruct((T, width), F32),
                   jax.ShapeDtypeStruct((width, T), BF16),
                   jax.ShapeDtypeStruct((T, width), BF16),
                   jax.ShapeDtypeStruct((width, T), BF16)],
        scratch_shapes=[pltpu.VMEM((ROW_TILE, D), BF16), pltpu.VMEM((ROW_TILE, D), BF16)],
        compiler_params=_params(("parallel",)),
        name="proj",
    )(h2d, w_u, w_qt, w_k, w_vt)


def _pool_kernel(u_ref, w_ref, s_ref, o_ref):
    S = u_ref.shape[0]
    row = lax.broadcasted_iota(jnp.int32, (S, LANES), 0)
    for g, win in enumerate(POOL_WINDOWS):
        u = u_ref[:, g * LANES:(g + 1) * LANES]
        acc = u
        k = 1
        while k < win:
            shifted = jnp.where(row >= k, pltpu.roll(acc, k, 0), 0.0)
            acc = acc + shifted
            k *= 2
        count = jnp.minimum(row + 1, win).astype(F32)
        pooled = acc / count - u
        mixed = jnp.dot(pooled.astype(BF16), w_ref[g], preferred_element_type=F32)
        o_ref[:, g * LANES:(g + 1) * LANES] = (
            mixed * s_ref[:, g * LANES:(g + 1) * LANES]).astype(o_ref.dtype)


def _pool(u, w_pool_b, pool_scale, B, S):
    T, W = u.shape
    return pl.pallas_call(
        _pool_kernel,
        grid=(B,),
        in_specs=[pl.BlockSpec((S, W), lambda b: (b, 0)),
                  pl.BlockSpec(w_pool_b.shape, lambda b: (0, 0, 0)),
                  pl.BlockSpec((1, W), lambda b: (0, 0))],
        out_specs=pl.BlockSpec((S, W), lambda b: (b, 0)),
        out_shape=jax.ShapeDtypeStruct((T, W), BF16),
        compiler_params=_params(("parallel",)),
        name="pool",
    )(u, w_pool_b, pool_scale)


def _attn_kernel(qt_ref, k_ref, vt_ref, o_ref, acc_ref, car_ref, z0_ref, z1_ref, a0_ref, a1_ref):
    i = pl.program_id(2)
    qb = QUERY_BLOCK
    kb = KEY_BLOCK
    width = 2 * qb
    pairs = range(PAIRS_PER_STEP)
    pair_rows = [slice(p * LANES, (p + 1) * LANES) for p in pairs]
    feat = lax.broadcasted_iota(jnp.int32, (LANES, qb), 0)
    rhs = []
    for p in pairs:
        q2 = qt_ref[pair_rows[p], :]
        zero = jnp.zeros_like(q2)
        rhs.append(jnp.concatenate([jnp.where(feat < SB_HEAD_DIM, q2, zero),
                                    jnp.where(feat >= SB_HEAD_DIM, q2, zero)], axis=1))
    sub = lax.broadcasted_iota(jnp.int32, (SUBLANES, width), 0)
    lane = lax.broadcasted_iota(jnp.int32, (SUBLANES, width), 1)
    q_pos = i * qb + jnp.bitwise_and(lane, qb - 1)

    acc_ref[...] = jnp.zeros_like(acc_ref)
    car_ref[...] = jnp.zeros_like(car_ref)

    n_diag = qb // kb
    last = (i + 1) * n_diag - 1

    def block_start(n):
        return pl.multiple_of(jnp.clip(last - n, 0, last) * kb, kb)

    def scores(n, z_ref):
        for p in pairs:
            z_ref[p] = jnp.dot(k_ref[pl.ds(block_start(n), kb), pair_rows[p]], rhs[p],
                               preferred_element_type=F32)

    def values(n, a_ref):
        for p in pairs:
            acc_ref[p] += jnp.dot(vt_ref[pair_rows[p], pl.ds(block_start(n), kb)], a_ref[p],
                                  preferred_element_type=F32)

    def weights(n, z_ref, a_ref, masked):
        for p in pairs:
            weights_one(n, z_ref.at[p], a_ref.at[p], car_ref.at[p], masked)

    def weights_one(n, z_ref, a_ref, car_ref, masked):
        start = block_start(n)
        zs, sps, sufs, masks = [], [], [], []
        run = jnp.zeros((SUBLANES, width), F32)
        for v in reversed(range(KEY_SEG)):
            zv = z_ref[v * SUBLANES:(v + 1) * SUBLANES, :]
            neg_abs = lax.bitcast_convert_type(
                lax.bitcast_convert_type(zv, jnp.uint32) | jnp.uint32(F32_SIGN_BIT), F32)
            sp = jnp.maximum(zv, 0.0) + jnp.log(1.0 + jnp.exp(neg_abs))
            if masked:
                m = (start + sub * KEY_SEG + v) < q_pos
                sp = jnp.where(m, sp, 0.0)
                masks.append(m)
            zs.append(zv)
            sps.append(sp)
            sufs.append(run)
            run = run + sp
        incl = run
        for step in (1, 2, 4):
            shifted = pltpu.roll(incl, SUBLANES - step, 0)
            incl = incl + jnp.where(sub < SUBLANES - step, shifted, 0.0)
        base = (incl - run) + car_ref[...]
        parts = []
        for idx in range(KEY_SEG):
            a = jnp.exp(zs[idx] - sps[idx] - (sufs[idx] + base))
            if masked:
                a = jnp.where(masks[idx], a, 0.0)
            parts.append(a)
        a_ref[...] = jnp.concatenate(parts[::-1], axis=0).astype(BF16)
        car_ref[...] += jnp.broadcast_to(incl[0:1, :], (SUBLANES, width))

    assert n_diag == 2
    scores(0, z0_ref)
    scores(1, z1_ref)
    weights(0, z0_ref, a0_ref, True)
    scores(2, z0_ref)
    weights(1, z1_ref, a1_ref, True)
    values(0, a0_ref)

    def pair(t, carry):
        n = 2 + 2 * t
        values(n - 1, a1_ref)
        weights(n, z0_ref, a0_ref, False)
        scores(n + 1, z1_ref)
        values(n, a0_ref)
        weights(n + 1, z1_ref, a1_ref, False)
        scores(n + 2, z0_ref)
        return carry

    lax.fori_loop(0, i, pair, 0)
    values(last, a1_ref)

    for p in pairs:
        acc = acc_ref[p]
        both = jnp.where(feat < SB_HEAD_DIM, acc[:, :qb], acc[:, qb:])
        o_ref[:, pair_rows[p]] = both.T.astype(o_ref.dtype)


def _attention(q_t, k_perm, v_t, B, S):
    W, T = q_t.shape
    qb = QUERY_BLOCK
    nq = S // qb
    step_w = PAIRS_PER_STEP * LANES
    groups = W // step_w
    per_pair = lambda shape, dtype: pltpu.VMEM((PAIRS_PER_STEP,) + shape, dtype)
    return pl.pallas_call(
        _attn_kernel,
        grid=(B, groups, nq),
        in_specs=[pl.BlockSpec((step_w, qb), lambda b, g, i: (g, b * nq + i)),
                  pl.BlockSpec((S, step_w), lambda b, g, i: (b, g)),
                  pl.BlockSpec((step_w, S), lambda b, g, i: (g, b))],
        out_specs=pl.BlockSpec((qb, step_w), lambda b, g, i: (b * nq + i, g)),
        out_shape=jax.ShapeDtypeStruct((T, W), BF16),
        scratch_shapes=[per_pair((LANES, 2 * qb), F32), per_pair((SUBLANES, 2 * qb), F32),
                        per_pair((KEY_BLOCK, 2 * qb), F32), per_pair((KEY_BLOCK, 2 * qb), F32),
                        per_pair((KEY_BLOCK, 2 * qb), BF16), per_pair((KEY_BLOCK, 2 * qb), BF16)],
        compiler_params=_params(("parallel", "parallel", "parallel")),
        name="attn",
    )(q_t, k_perm, v_t)


def _layer_norm(y, g, b):
    mu = jnp.mean(y, axis=-1, keepdims=True)
    yc = y - mu
    var = jnp.mean(yc * yc, axis=-1, keepdims=True)
    return yc * lax.rsqrt(var + LN_EPS) * g + b


def _mix_kernel(alpha, yp_ref, ys_ref, h_ref, wo_ref, g_ref, b_ref, wr_ref, br_ref,
                h1_ref, route_ref, cnt_ref, run_ref):
    @pl.when(pl.program_id(0) == 0)
    def _():
        run_ref[...] = jnp.zeros_like(run_ref)

    half = yp_ref.shape[1]
    mix = (jnp.dot(yp_ref[...], wo_ref[:half, :], preferred_element_type=F32) +
           jnp.dot(ys_ref[...], wo_ref[half:, :], preferred_element_type=F32))
    h1 = _layer_norm(alpha * h_ref[...] + mix, g_ref[...], b_ref[...])
    h1_ref[...] = h1

    hh = h1.astype(BF16)
    hl = (h1 - hh.astype(F32)).astype(BF16)
    prod_h = jnp.dot(hh, wr_ref[...], preferred_element_type=F32)
    prod_l = jnp.dot(hl, wr_ref[:, :LANES], preferred_element_type=F32)
    logits = prod_h[:, :LANES] + prod_l + prod_h[:, LANES:] + br_ref[...]
    col = lax.broadcasted_iota(jnp.int32, logits.shape, 1)
    gl = jnp.where(col < N_EXPERT_GROUPS, logits, NEG_BIG)
    gmax = jnp.max(gl, axis=-1, keepdims=True)
    gidx = jnp.min(jnp.where(gl == gmax, col, LANES), axis=-1, keepdims=True)
    g_p = 1.0 / jnp.sum(jnp.exp(gl - gmax), axis=-1, keepdims=True)
    lo_col = N_EXPERT_GROUPS + EXPERTS_PER_GROUP * gidx
    el = jnp.where(col >= lo_col, jnp.where(col < lo_col + EXPERTS_PER_GROUP, logits, NEG_BIG),
                   NEG_BIG)
    m1 = jnp.max(el, axis=-1, keepdims=True)
    i1 = jnp.min(jnp.where(el == m1, col, LANES), axis=-1, keepdims=True)
    el2 = jnp.where(col == i1, NEG_BIG, el)
    m2 = jnp.max(el2, axis=-1, keepdims=True)
    i2 = jnp.min(jnp.where(el2 == m2, col, LANES), axis=-1, keepdims=True)
    ratio = jnp.exp(m2 - m1)
    gate1 = g_p / (1.0 + ratio)
    gate2 = g_p * ratio / (1.0 + ratio)
    e1 = (i1 - N_EXPERT_GROUPS).astype(F32)
    e2 = (i2 - N_EXPERT_GROUPS).astype(F32)

    rows = logits.shape[0]
    hit1 = col == i1
    hit2 = col == i2
    onehot = jnp.where(hit1, 1.0, jnp.where(hit2, 1.0, 0.0))
    r_i = lax.broadcasted_iota(jnp.int32, (rows, rows), 0)
    c_i = lax.broadcasted_iota(jnp.int32, (rows, rows), 1)
    earlier = jnp.where(c_i < r_i, 1.0, 0.0).astype(BF16)
    before = run_ref[...] + jnp.dot(earlier, onehot.astype(BF16), preferred_element_type=F32)
    rank1 = jnp.sum(jnp.where(hit1, before, 0.0), axis=-1, keepdims=True)
    rank2 = jnp.sum(jnp.where(hit2, before, 0.0), axis=-1, keepdims=True)
    run_ref[...] += jnp.sum(onehot, axis=0, keepdims=True)
    cnt_ref[...] = run_ref[...]

    route_ref[...] = jnp.where(col == 0, e1, jnp.where(col == 1, e2,
                               jnp.where(col == 2, gate1, jnp.where(col == 3, gate2,
                               jnp.where(col == 4, rank1, jnp.where(col == 5, rank2, 0.0))))))


def _mix_norm_route(alpha, y_pool, y_sb, h2d, w_out_b, g, b, w_r2, b_r):
    T, D = h2d.shape
    half = y_pool.shape[1]
    full = lambda a: pl.BlockSpec(a.shape, lambda i: (0,) * a.ndim)
    return pl.pallas_call(
        functools.partial(_mix_kernel, alpha),
        grid=(T // ROW_TILE,),
        in_specs=[pl.BlockSpec((ROW_TILE, half), lambda i: (i, 0)),
                  pl.BlockSpec((ROW_TILE, half), lambda i: (i, 0)),
                  pl.BlockSpec((ROW_TILE, D), lambda i: (i, 0)),
                  full(w_out_b), full(g), full(b), full(w_r2), full(b_r)],
        out_specs=[pl.BlockSpec((ROW_TILE, D), lambda i: (i, 0)),
                   pl.BlockSpec((ROW_TILE, LANES), lambda i: (i, 0)),
                   pl.BlockSpec((1, LANES), lambda i: (0, 0))],
        out_shape=[jax.ShapeDtypeStruct((T, D), F32), jax.ShapeDtypeStruct((T, LANES), F32),
                   jax.ShapeDtypeStruct((1, LANES), F32)],
        scratch_shapes=[pltpu.VMEM((1, LANES), F32)],
        compiler_params=_params(("arbitrary",)),
        name="mix_norm_route",
    )(y_pool, y_sb, h2d, w_out_b, g, b, w_r2, b_r)


def _row_gather_copy(src_hbm, row, dst_ref, dst_row, sem):
    return pltpu.make_async_copy(src_hbm.at[pl.ds(row, 1)], dst_ref.at[pl.ds(dst_row, 1)], sem)


def _row_scatter_copy(src_ref, src_row, dst_hbm, row, sem):
    return pltpu.make_async_copy(src_ref.at[pl.ds(src_row, 1)], dst_hbm.at[pl.ds(row, 1)], sem)


def _dispatch_kernel(pos_ref, tail_ref, nvb_ref, h_ref, xs_hbm, sbuf, zbuf, sem, zsem):
    i = pl.program_id(0)
    nb = pl.num_programs(0)
    rows = DISPATCH_TILE
    blk = DISPATCH_ROWS
    n_blocks = xs_hbm.shape[0] // blk

    def zero_copy(start):
        return pltpu.make_async_copy(zbuf, xs_hbm.at[pl.ds(pl.multiple_of(start, blk), blk)], zsem)

    @pl.when(i == 0)
    def _():
        zbuf[...] = jnp.zeros_like(zbuf)
        for e in range(N_EXPERTS):
            @pl.when(tail_ref[e] >= 0)
            def _():
                zero_copy(tail_ref[e]).start()

        def start_unused(b, carry):
            zero_copy(b * blk).start()
            return carry

        def wait_unused(b, carry):
            zero_copy(b * blk).wait()
            return carry

        lax.fori_loop(nvb_ref[0], n_blocks, start_unused, 0)
        for e in range(N_EXPERTS):
            @pl.when(tail_ref[e] >= 0)
            def _():
                zero_copy(tail_ref[e]).wait()
        lax.fori_loop(nvb_ref[0], n_blocks, wait_unused, 0)

    slot = i % 2

    def wait_slot(s):
        for _ in range(TOP_K):
            pltpu.make_async_copy(sbuf.at[s], xs_hbm.at[pl.ds(0, rows)], sem.at[s]).wait()

    @pl.when(i >= 2)
    def _():
        wait_slot(slot)

    sbuf[slot] = h_ref[...]

    def body(r, carry):
        for k in range(TOP_K):
            _row_scatter_copy(sbuf.at[slot], r, xs_hbm, pos_ref[(i * rows + r) * TOP_K + k],
                              sem.at[slot]).start()
        return carry

    lax.fori_loop(0, rows, body, 0, unroll=8)

    @pl.when(i == nb - 1)
    def _():
        @pl.when(nb >= 2)
        def _():
            wait_slot(1 - slot)
        wait_slot(slot)


def _dispatch(pos, tail, n_valid_blocks, h1, n_blocks):
    T, D = h1.shape
    rows = DISPATCH_TILE
    grid_spec = pltpu.PrefetchScalarGridSpec(
        num_scalar_prefetch=3,
        grid=(T // rows,),
        in_specs=[pl.BlockSpec((rows, D), lambda i, pos, tail, nvb: (i, 0))],
        out_specs=pl.BlockSpec(memory_space=pl.ANY),
        scratch_shapes=[pltpu.VMEM((2, rows, D), F32), pltpu.VMEM((DISPATCH_ROWS, D), F32),
                        pltpu.SemaphoreType.DMA((2,)), pltpu.SemaphoreType.DMA],
    )
    return pl.pallas_call(
        _dispatch_kernel,
        grid_spec=grid_spec,
        out_shape=jax.ShapeDtypeStruct((n_blocks * DISPATCH_ROWS, D), F32),
        compiler_params=_params(("arbitrary",)),
        name="dispatch",
    )(pos, tail, n_valid_blocks, h1)


def _expert_kernel(be_ref, nvb_ref, xs_ref, wg_ref, wu_ref, wd_ref, ys_ref, wgb, wub, wdb):
    i = pl.program_id(0)
    e = be_ref[i]
    prev = be_ref[jnp.maximum(i - 1, 0)]

    @pl.when(jnp.logical_or(i == 0, e != prev))
    def _():
        wgb[...] = wg_ref[...].astype(BF16)
        wub[...] = wu_ref[...].astype(BF16)
        wdb[...] = wd_ref[...].astype(BF16)

    @pl.when(i < nvb_ref[0])
    def _():
        x = xs_ref[...].astype(BF16)
        gate = jnp.dot(x, wgb[...], preferred_element_type=F32)
        up = jnp.dot(x, wub[...], preferred_element_type=F32)
        hid = (gate * jax.nn.sigmoid(gate) * up).astype(BF16)
        ys_ref[...] = jnp.dot(hid, wdb[...], preferred_element_type=F32)

    @pl.when(i >= nvb_ref[0])
    def _():
        ys_ref[...] = jnp.zeros_like(ys_ref)


def _experts(layer, block_e, n_valid_blocks, xs, w_gate, w_up, w_down):
    P, D = xs.shape
    DE = w_gate.shape[-1]
    rows = DISPATCH_ROWS
    n_blocks = P // rows
    x_map = lambda i, be, nvb: (jnp.minimum(i, jnp.maximum(nvb[0] - 1, 0)), 0)
    grid_spec = pltpu.PrefetchScalarGridSpec(
        num_scalar_prefetch=2,
        grid=(n_blocks,),
        in_specs=[pl.BlockSpec((rows, D), x_map),
                  pl.BlockSpec((None, None, D, DE), lambda i, be, nvb: (layer, be[i], 0, 0)),
                  pl.BlockSpec((None, None, D, DE), lambda i, be, nvb: (layer, be[i], 0, 0)),
                  pl.BlockSpec((None, None, DE, D), lambda i, be, nvb: (layer, be[i], 0, 0))],
        out_specs=pl.BlockSpec((rows, D), lambda i, be, nvb: (i, 0)),
        scratch_shapes=[pltpu.VMEM((D, DE), BF16), pltpu.VMEM((D, DE), BF16),
                        pltpu.VMEM((DE, D), BF16)],
    )
    return pl.pallas_call(
        _expert_kernel,
        grid_spec=grid_spec,
        out_shape=jax.ShapeDtypeStruct((P, D), F32),
        compiler_params=_params(("arbitrary",)),
        name="experts",
    )(block_e, n_valid_blocks, xs, w_gate, w_up, w_down)


def _combine_kernel(alpha, with_proj, pos_ref, ys_hbm, h_ref, route_ref, g_ref, b_ref, *refs):
    if with_proj:
        w_refs, o_ref, out_refs = refs[0:4], refs[4], refs[5:9]
        buf_a, buf_b, sem, fence_sem, xb_ref, xp_ref = refs[9:]
    else:
        o_ref, buf_a, buf_b, sem, fence_sem = refs
    i = pl.program_id(0)
    nb = pl.num_programs(0)
    tile = o_ref.shape[0]
    half = tile // 2
    bufs = (buf_a, buf_b)

    def issue(tile_idx, part, lo, hi):
        base = tile_idx * tile + part * half
        for r in range(lo, hi):
            for k in range(TOP_K):
                _row_gather_copy(ys_hbm, pos_ref[(base + r) * TOP_K + k], bufs[part].at[k], r,
                                 sem.at[part]).start()

    def wait(part):
        for k in range(TOP_K):
            pltpu.make_async_copy(ys_hbm.at[pl.ds(0, half)], bufs[part].at[k],
                                  sem.at[part]).wait()

    def finish(part):
        rows = slice(part * half, (part + 1) * half)
        route = route_ref[rows, :]
        moe = route[:, 2:3] * bufs[part][0] + route[:, 3:4] * bufs[part][1]
        o_ref[rows, :] = _layer_norm(alpha * h_ref[rows, :] + moe, g_ref[...], b_ref[...])

    def fence():
        pl.semaphore_signal(fence_sem, 1)
        pl.semaphore_wait(fence_sem, 1)

    @pl.when(i == 0)
    def _():
        issue(0, 0, 0, half)
        issue(0, 1, 0, half)

    nxt = jnp.minimum(i + 1, nb - 1)
    wait(0)
    wait(1)
    finish(0)
    fence()
    if with_proj:
        stages = _projection_stages(o_ref, w_refs, out_refs, xb_ref, xp_ref)
        issue(nxt, 0, 0, half // 2)
        finish(1)
        todo = [(0, r) for r in range(half // 2, half)] + [(1, r) for r in range(half)]
        per_stage = -(-len(todo) // len(stages))
        for n, stage in enumerate(stages):
            fence()
            for part, r in todo[n * per_stage:(n + 1) * per_stage]:
                issue(nxt, part, r, r + 1)
            stage()
    else:
        issue(nxt, 0, 0, half)
        finish(1)
        fence()
        issue(nxt, 1, 0, half)

    @pl.when(i == nb - 1)
    def _():
        wait(0)
        wait(1)


def _combine_norm(alpha, pos, ys, h1, route, g, b, proj_weights=None):
    T, D = h1.shape
    rows = ROW_TILE
    with_proj = proj_weights is not None
    const = lambda a: pl.BlockSpec(a.shape, lambda i, pos: (0,) * a.ndim)
    in_specs = [pl.BlockSpec(memory_space=pl.ANY),
                pl.BlockSpec((rows, D), lambda i, pos: (i, 0)),
                pl.BlockSpec((rows, LANES), lambda i, pos: (i, 0)),
                const(g), const(b)]
    out_specs = [pl.BlockSpec((rows, D), lambda i, pos: (i, 0))]
    out_shape = [jax.ShapeDtypeStruct((T, D), F32)]
    args = [pos, ys, h1, route, g, b]
    if with_proj:
        width = proj_weights[0].shape[1]
        in_specs += [const(w) for w in proj_weights]
        args += list(proj_weights)
        row_major = lambda: pl.BlockSpec((rows, width), lambda i, pos: (i, 0))
        feat_major = lambda: pl.BlockSpec((width, rows), lambda i, pos: (0, i))
        out_specs += [row_major(), feat_major(), row_major(), feat_major()]
        out_shape += [jax.ShapeDtypeStruct((T, width), F32), jax.ShapeDtypeStruct((width, T), BF16),
                      jax.ShapeDtypeStruct((T, width), BF16), jax.ShapeDtypeStruct((width, T), BF16)]
    grid_spec = pltpu.PrefetchScalarGridSpec(
        num_scalar_prefetch=1,
        grid=(T // rows,),
        in_specs=in_specs,
        out_specs=out_specs,
        scratch_shapes=[pltpu.VMEM((TOP_K, rows // 2, D), F32), pltpu.VMEM((TOP_K, rows // 2, D), F32),
                        pltpu.SemaphoreType.DMA((2,)), pltpu.SemaphoreType.REGULAR] +
                       ([pltpu.VMEM((rows, D), BF16), pltpu.VMEM((rows, D), BF16)] if with_proj else []),
    )
    out = pl.pallas_call(
        functools.partial(_combine_kernel, alpha, with_proj),
        grid_spec=grid_spec,
        out_shape=out_shape,
        compiler_params=_params(("arbitrary",)),
        name="combine_norm_proj" if with_proj else "combine_norm",
    )(*args)
    return out if with_proj else out[0]


def _dispatch_plan(route, counts_row, n_blocks):
    rows = DISPATCH_ROWS
    experts = jnp.arange(N_EXPERTS, dtype=jnp.int32)
    counts = counts_row[0, N_EXPERT_GROUPS:N_EXPERT_GROUPS + N_EXPERTS].astype(jnp.int32)
    padded = ((counts + rows - 1) // rows) * rows
    pend = jnp.cumsum(padded)
    pstart = pend - padded
    block_start = jnp.arange(n_blocks, dtype=jnp.int32) * rows
    block_e = jnp.minimum(jnp.sum((block_start[:, None] >= pend[None, :]).astype(jnp.int32), axis=1),
                          N_EXPERTS - 1).astype(jnp.int32)
    n_valid_blocks = (pend[-1:] // rows).astype(jnp.int32)
    tail = jnp.where(counts > 0, pend - rows, -1).astype(jnp.int32)
    e = route[:, :TOP_K].astype(jnp.int32)
    rank = route[:, 4:4 + TOP_K].astype(jnp.int32)
    base = jnp.sum(jnp.where(e[:, :, None] == experts[None, None, :], pstart[None, None, :], 0),
                   axis=-1)
    pos = (base + rank).reshape(-1).astype(jnp.int32)
    return block_e, n_valid_blocks, tail, pos


def kernel(x, w_in, w_pool, pool_scale, w_out, ln1_g, ln1_b, w_router_group, b_router_group,
           w_router_expert, b_router_expert, w_gate, w_up, w_down, ln2_g, ln2_b):
    B, S, D = x.shape
    depth = w_in.shape[0]
    T = B * S
    alpha = (2.0 * depth) ** 0.25
    pool_width = w_pool.shape[1] * w_pool.shape[2]
    sb_width = (w_in.shape[2] - pool_width) // 3
    n_blocks = -(-T * TOP_K // DISPATCH_ROWS) + N_EXPERTS

    col_scale = jnp.concatenate([jnp.ones((pool_width,), F32),
                                 jnp.full((sb_width,), SB_HEAD_DIM ** -0.5, F32),
                                 jnp.ones((2 * sb_width,), F32)])
    w_in_s = w_in * col_scale
    w_u = w_in_s[:, :, :pool_width].astype(BF16)
    w_qt = jnp.swapaxes(w_in_s[:, :, pool_width:pool_width + sb_width], 1, 2).astype(BF16)
    w_k = w_in_s[:, :, pool_width + sb_width:pool_width + 2 * sb_width].astype(BF16)
    w_vt = jnp.swapaxes(w_in_s[:, :, pool_width + 2 * sb_width:], 1, 2).astype(BF16)
    w_pool_b = w_pool.astype(BF16)
    w_out_b = w_out.astype(BF16)
    w_r = jnp.concatenate([w_router_group, w_router_expert,
                           jnp.zeros((depth, D, LANES - N_EXPERT_GROUPS - N_EXPERTS), F32)], axis=-1)
    wr_hi = w_r.astype(BF16)
    wr_lo = (w_r - wr_hi.astype(F32)).astype(BF16)
    w_r2 = jnp.concatenate([wr_hi, wr_lo], axis=-1)
    b_r = jnp.concatenate([b_router_group.astype(F32),
                           b_router_expert.astype(F32).reshape(depth, N_EXPERTS),
                           jnp.zeros((depth, LANES - N_EXPERT_GROUPS - N_EXPERTS), F32)], axis=-1)

    h = x.reshape(T, D)
    u, q_t, k_perm, v_t = _project(h, w_u[0], w_qt[0], w_k[0], w_vt[0])
    for l in range(depth):
        y_pool = _pool(u, w_pool_b[l], pool_scale[l][None, :], B, S)
        y_sb = _attention(q_t, k_perm, v_t, B, S)
        h1, route, counts = _mix_norm_route(alpha, y_pool, y_sb, h, w_out_b[l], ln1_g[l][None, :],
                                            ln1_b[l][None, :], w_r2[l], b_r[l][None, :])
        block_e, n_valid_blocks, tail, pos = _dispatch_plan(route, counts, n_blocks)
        xs = _dispatch(pos, tail, n_valid_blocks, h1, n_blocks)
        ys = _experts(l, block_e, n_valid_blocks, xs, w_gate, w_up, w_down)
        nxt = None if l + 1 == depth else (w_u[l + 1], w_qt[l + 1], w_k[l + 1], w_vt[l + 1])
        out = _combine_norm(alpha, pos, ys, h1, route, ln2_g[l][None, :], ln2_b[l][None, :], nxt)
        if nxt is None:
            h = out
        else:
            h, u, q_t, k_perm, v_t = out
    return h.reshape(B, S, D)
```

```python
import functools

import jax
import jax.numpy as jnp
from jax import lax
from jax.experimental import pallas as pl
from jax.experimental.pallas import tpu as pltpu

F32 = jnp.float32
BF16 = jnp.bfloat16

POOL_WINDOWS = (2, 4, 8, 16)
SB_HEAD_DIM = 64
N_EXPERT_GROUPS = 4
EXPERTS_PER_GROUP = 8
N_EXPERTS = N_EXPERT_GROUPS * EXPERTS_PER_GROUP
TOP_K = 2
LN_EPS = 1e-5

LANES = 128
SUBLANES = 8
ROW_TILE = 512
QUERY_BLOCK = 256
KEY_BLOCK = 128
KEY_SEG = KEY_BLOCK // SUBLANES
PAIRS_PER_STEP = 2
DISPATCH_ROWS = 256
DISPATCH_TILE = 256
NEG_BIG = -1e30
VMEM_LIMIT = 48 * 1024 * 1024


def _params(sem):
    return pltpu.CompilerParams(dimension_semantics=sem, vmem_limit_bytes=VMEM_LIMIT)


def _proj_kernel(x_ref, wu_ref, wqt_ref, wk_ref, wvt_ref, u_ref, qt_ref, k_ref, vt_ref,
                 xb_ref, xp_ref):
    for stage in _projection_stages(x_ref, (wu_ref, wqt_ref, wk_ref, wvt_ref),
                                    (u_ref, qt_ref, k_ref, vt_ref), xb_ref, xp_ref):
        stage()


def _projection_stages(x_ref, w_refs, out_refs, xb_ref, xp_ref):
    wu_ref, wqt_ref, wk_ref, wvt_ref = w_refs
    u_ref, qt_ref, k_ref, vt_ref = out_refs
    rows = x_ref.shape[0]
    nt = (((1,), (1,)), ((), ()))

    def pool_input():
        xb_ref[...] = x_ref[...].astype(BF16)
        u_ref[...] = jnp.dot(xb_ref[...], wu_ref[...], preferred_element_type=F32)

    def queries():
        qt_ref[...] = lax.dot_general(wqt_ref[...], xb_ref[...], nt,
                                      preferred_element_type=F32).astype(qt_ref.dtype)

    def keys():
        p_i = lax.broadcasted_iota(jnp.int32, (KEY_BLOCK, KEY_BLOCK), 0)
        t_i = lax.broadcasted_iota(jnp.int32, (KEY_BLOCK, KEY_BLOCK), 1)
        src = (p_i % SUBLANES) * KEY_SEG + p_i // SUBLANES
        perm = jnp.where(t_i == src, 1.0, 0.0).astype(BF16)
        for g in range(rows // KEY_BLOCK):
            blk = slice(g * KEY_BLOCK, (g + 1) * KEY_BLOCK)
            xp_ref[blk, :] = jnp.dot(perm, xb_ref[blk, :],
                                     preferred_element_type=F32).astype(BF16)
        k_ref[...] = jnp.dot(xp_ref[...], wk_ref[...],
                             preferred_element_type=F32).astype(k_ref.dtype)

    def values():
        vt_ref[...] = lax.dot_general(wvt_ref[...], xp_ref[...], nt,
                                      preferred_element_type=F32).astype(vt_ref.dtype)

    return pool_input, queries, keys, values


def _project(h2d, w_u, w_qt, w_k, w_vt):
    T, D = h2d.shape
    width = w_u.shape[1]
    row_major = lambda: pl.BlockSpec((ROW_TILE, width), lambda i: (i, 0))
    feat_major = lambda: pl.BlockSpec((width, ROW_TILE), lambda i: (0, i))
    full = lambda a: pl.BlockSpec(a.shape, lambda i: (0, 0))
    return pl.pallas_call(
        _proj_kernel,
        grid=(T // ROW_TILE,),
        in_specs=[pl.BlockSpec((ROW_TILE, D), lambda i: (i, 0)),
                  full(w_u), full(w_qt), full(w_k), full(w_vt)],
        out_specs=[row_major(), feat_major(), row_major(), feat_major()],
        out_shape=[jax.ShapeDtypeStruct((T, width), F32),
                   jax.ShapeDtypeStruct((width, T), BF16),
                   jax.ShapeDtypeStruct((T, width), BF16),
                   jax.ShapeDtypeStruct((width, T), BF16)],
        scratch_shapes=[pltpu.VMEM((ROW_TILE, D), BF16), pltpu.VMEM((ROW_TILE, D), BF16)],
        compiler_params=_params(("parallel",)),
        name="proj",
    )(h2d, w_u, w_qt, w_k, w_vt)


def _pool_kernel(u_ref, w_ref, s_ref, o_ref):
    S = u_ref.shape[0]
    row = lax.broadcasted_iota(jnp.int32, (S, LANES), 0)
    for g, win in enumerate(POOL_WINDOWS):
        u = u_ref[:, g * LANES:(g + 1) * LANES]
        acc = u
        k = 1
        while k < win:
            shifted = jnp.where(row >= k, pltpu.roll(acc, k, 0), 0.0)
            acc = acc + shifted
            k *= 2
        count = jnp.minimum(row + 1, win).astype(F32)
        pooled = acc / count - u
        mixed = jnp.dot(pooled.astype(BF16), w_ref[g], preferred_element_type=F32)
        o_ref[:, g * LANES:(g + 1) * LANES] = (
            mixed * s_ref[:, g * LANES:(g + 1) * LANES]).astype(o_ref.dtype)


def _pool(u, w_pool_b, pool_scale, B, S):
    T, W = u.shape
    return pl.pallas_call(
        _pool_kernel,
        grid=(B,),
        in_specs=[pl.BlockSpec((S, W), lambda b: (b, 0)),
                  pl.BlockSpec(w_pool_b.shape, lambda b: (0, 0, 0)),
                  pl.BlockSpec((1, W), lambda b: (0, 0))],
        out_specs=pl.BlockSpec((S, W), lambda b: (b, 0)),
        out_shape=jax.ShapeDtypeStruct((T, W), BF16),
        compiler_params=_params(("parallel",)),
        name="pool",
    )(u, w_pool_b, pool_scale)


def _attn_kernel(qt_ref, k_ref, vt_ref, o_ref, acc_ref, car_ref, z0_ref, z1_ref, a0_ref, a1_ref):
    i = pl.program_id(2)
    qb = QUERY_BLOCK
    kb = KEY_BLOCK
    width = 2 * qb
    pairs = range(PAIRS_PER_STEP)
    pair_rows = [slice(p * LANES, (p + 1) * LANES) for p in pairs]
    feat = lax.broadcasted_iota(jnp.int32, (LANES, qb), 0)
    rhs = []
    for p in pairs:
        q2 = qt_ref[pair_rows[p], :]
        zero = jnp.zeros_like(q2)
        rhs.append(jnp.concatenate([jnp.where(feat < SB_HEAD_DIM, q2, zero),
                                    jnp.where(feat >= SB_HEAD_DIM, q2, zero)], axis=1))
    sub = lax.broadcasted_iota(jnp.int32, (SUBLANES, width), 0)
    lane = lax.broadcasted_iota(jnp.int32, (SUBLANES, width), 1)
    q_pos = i * qb + jnp.bitwise_and(lane, qb - 1)

    acc_ref[...] = jnp.zeros_like(acc_ref)
    car_ref[...] = jnp.ones_like(car_ref)

    n_diag = qb // kb
    last = (i + 1) * n_diag - 1

    def block_start(n):
        return pl.multiple_of(jnp.clip(last - n, 0, last) * kb, kb)

    def scores(n, z_ref):
        for p in pairs:
            z_ref[p] = jnp.dot(k_ref[pl.ds(block_start(n), kb), pair_rows[p]], rhs[p],
                               preferred_element_type=F32)

    def values(n, a_ref):
        for p in pairs:
            acc_ref[p] += jnp.dot(vt_ref[pair_rows[p], pl.ds(block_start(n), kb)], a_ref[p],
                                  preferred_element_type=F32)

    def weights(n, z_ref, a_ref, masked):
        for p in pairs:
            weights_one(n, z_ref.at[p], a_ref.at[p], car_ref.at[p], masked)

    def weights_one(n, z_ref, a_ref, car_ref, masked):
        start = block_start(n)
        run = jnp.ones((SUBLANES, width), F32)
        for v in reversed(range(KEY_SEG)):
            slab = slice(v * SUBLANES, (v + 1) * SUBLANES)
            beta = 1.0 / (1.0 + jnp.exp(z_ref[slab, :]))
            rest = 1.0 - beta
            if masked:
                m = (start + sub * KEY_SEG + v) < q_pos
                beta = jnp.where(m, beta, 0.0)
                rest = jnp.where(m, rest, 1.0)
            z_ref[slab, :] = beta * run
            run = run * rest
        incl = run
        for step in (1, 2, 4):
            shifted = pltpu.roll(incl, SUBLANES - step, 0)
            incl = incl * jnp.where(sub < SUBLANES - step, shifted, 1.0)
        later = jnp.where(sub < SUBLANES - 1, pltpu.roll(incl, SUBLANES - 1, 0), 1.0)
        base = later * car_ref[...]
        a_ref[...] = jnp.concatenate(
            [z_ref[v * SUBLANES:(v + 1) * SUBLANES, :] * base for v in range(KEY_SEG)],
            axis=0).astype(BF16)
        car_ref[...] *= jnp.broadcast_to(incl[0:1, :], (SUBLANES, width))

    assert n_diag == 2
    scores(0, z0_ref)
    scores(1, z1_ref)
    weights(0, z0_ref, a0_ref, True)
    scores(2, z0_ref)
    weights(1, z1_ref, a1_ref, True)
    values(0, a0_ref)

    def pair(t, carry):
        n = 2 + 2 * t
        values(n - 1, a1_ref)
        weights(n, z0_ref, a0_ref, False)
        scores(n + 1, z1_ref)
        values(n, a0_ref)
        weights(n + 1, z1_ref, a1_ref, False)
        scores(n + 2, z0_ref)
        return carry

    lax.fori_loop(0, i, pair, 0)
    values(last, a1_ref)

    for p in pairs:
        acc = acc_ref[p]
        both = jnp.where(feat < SB_HEAD_DIM, acc[:, :qb], acc[:, qb:])
        o_ref[:, pair_rows[p]] = both.T.astype(o_ref.dtype)


def _attention(q_t, k_perm, v_t, B, S):
    W, T = q_t.shape
    qb = QUERY_BLOCK
    nq = S // qb
    step_w = PAIRS_PER_STEP * LANES
    groups = W // step_w
    per_pair = lambda shape, dtype: pltpu.VMEM((PAIRS_PER_STEP,) + shape, dtype)
    return pl.pallas_call(
        _attn_kernel,
        grid=(B, groups, nq),
        in_specs=[pl.BlockSpec((step_w, qb), lambda b, g, i: (g, b * nq + i)),
                  pl.BlockSpec((S, step_w), lambda b, g, i: (b, g)),
                  pl.BlockSpec((step_w, S), lambda b, g, i: (g, b))],
        out_specs=pl.BlockSpec((qb, step_w), lambda b, g, i: (b * nq + i, g)),
        out_shape=jax.ShapeDtypeStruct((T, W), BF16),
        scratch_shapes=[per_pair((LANES, 2 * qb), F32), per_pair((SUBLANES, 2 * qb), F32),
                        per_pair((KEY_BLOCK, 2 * qb), F32), per_pair((KEY_BLOCK, 2 * qb), F32),
                        per_pair((KEY_BLOCK, 2 * qb), BF16), per_pair((KEY_BLOCK, 2 * qb), BF16)],
        compiler_params=_params(("parallel", "parallel", "parallel")),
        name="attn",
    )(q_t, k_perm, v_t)


def _layer_norm(y, g, b):
    mu = jnp.mean(y, axis=-1, keepdims=True)
    yc = y - mu
    var = jnp.mean(yc * yc, axis=-1, keepdims=True)
    return yc * lax.rsqrt(var + LN_EPS) * g + b


def _mix_kernel(alpha, yp_ref, ys_ref, h_ref, wo_ref, g_ref, b_ref, wr_ref, br_ref,
                h1_ref, route_ref, cnt_ref, run_ref):
    @pl.when(pl.program_id(0) == 0)
    def _():
        run_ref[...] = jnp.zeros_like(run_ref)

    half = yp_ref.shape[1]
    mix = (jnp.dot(yp_ref[...], wo_ref[:half, :], preferred_element_type=F32) +
           jnp.dot(ys_ref[...], wo_ref[half:, :], preferred_element_type=F32))
    h1 = _layer_norm(alpha * h_ref[...] + mix, g_ref[...], b_ref[...])
    h1_ref[...] = h1

    hh = h1.astype(BF16)
    hl = (h1 - hh.astype(F32)).astype(BF16)
    prod_h = jnp.dot(hh, wr_ref[...], preferred_element_type=F32)
    prod_l = jnp.dot(hl, wr_ref[:, :LANES], preferred_element_type=F32)
    logits = prod_h[:, :LANES] + prod_l + prod_h[:, LANES:] + br_ref[...]
    col = lax.broadcasted_iota(jnp.int32, logits.shape, 1)
    gl = jnp.where(col < N_EXPERT_GROUPS, logits, NEG_BIG)
    gmax = jnp.max(gl, axis=-1, keepdims=True)
    gidx = jnp.min(jnp.where(gl == gmax, col, LANES), axis=-1, keepdims=True)
    g_p = 1.0 / jnp.sum(jnp.exp(gl - gmax), axis=-1, keepdims=True)
    lo_col = N_EXPERT_GROUPS + EXPERTS_PER_GROUP * gidx
    el = jnp.where(col >= lo_col, jnp.where(col < lo_col + EXPERTS_PER_GROUP, logits, NEG_BIG),
                   NEG_BIG)
    m1 = jnp.max(el, axis=-1, keepdims=True)
    i1 = jnp.min(jnp.where(el == m1, col, LANES), axis=-1, keepdims=True)
    el2 = jnp.where(col == i1, NEG_BIG, el)
    m2 = jnp.max(el2, axis=-1, keepdims=True)
    i2 = jnp.min(jnp.where(el2 == m2, col, LANES), axis=-1, keepdims=True)
    ratio = jnp.exp(m2 - m1)
    gate1 = g_p / (1.0 + ratio)
    gate2 = g_p * ratio / (1.0 + ratio)
    e1 = (i1 - N_EXPERT_GROUPS).astype(F32)
    e2 = (i2 - N_EXPERT_GROUPS).astype(F32)

    rows = logits.shape[0]
    hit1 = col == i1
    hit2 = col == i2
    onehot = jnp.where(hit1, 1.0, jnp.where(hit2, 1.0, 0.0))
    r_i = lax.broadcasted_iota(jnp.int32, (rows, rows), 0)
    c_i = lax.broadcasted_iota(jnp.int32, (rows, rows), 1)
    earlier = jnp.where(c_i < r_i, 1.0, 0.0).astype(BF16)
    before = run_ref[...] + jnp.dot(earlier, onehot.astype(BF16), preferred_element_type=F32)
    rank1 = jnp.sum(jnp.where(hit1, before, 0.0), axis=-1, keepdims=True)
    rank2 = jnp.sum(jnp.where(hit2, before, 0.0), axis=-1, keepdims=True)
    run_ref[...] += jnp.sum(onehot, axis=0, keepdims=True)
    cnt_ref[...] = run_ref[...]

    route_ref[...] = jnp.where(col == 0, e1, jnp.where(col == 1, e2,
                               jnp.where(col == 2, gate1, jnp.where(col == 3, gate2,
                               jnp.where(col == 4, rank1, jnp.where(col == 5, rank2, 0.0))))))


def _mix_norm_route(alpha, y_pool, y_sb, h2d, w_out_b, g, b, w_r2, b_r):
    T, D = h2d.shape
    half = y_pool.shape[1]
    full = lambda a: pl.BlockSpec(a.shape, lambda i: (0,) * a.ndim)
    return pl.pallas_call(
        functools.partial(_mix_kernel, alpha),
        grid=(T // ROW_TILE,),
        in_specs=[pl.BlockSpec((ROW_TILE, half), lambda i: (i, 0)),
                  pl.BlockSpec((ROW_TILE, half), lambda i: (i, 0)),
                  pl.BlockSpec((ROW_TILE, D), lambda i: (i, 0)),
                  full(w_out_b), full(g), full(b), full(w_r2), full(b_r)],
        out_specs=[pl.BlockSpec((ROW_TILE, D), lambda i: (i, 0)),
                   pl.BlockSpec((ROW_TILE, LANES), lambda i: (i, 0)),
                   pl.BlockSpec((1, LANES), lambda i: (0, 0))],
        out_shape=[jax.ShapeDtypeStruct((T, D), F32), jax.ShapeDtypeStruct((T, LANES), F32),
                   jax.ShapeDtypeStruct((1, LANES), F32)],
        scratch_shapes=[pltpu.VMEM((1, LANES), F32)],
        compiler_params=_params(("arbitrary",)),
        name="mix_norm_route",
    )(y_pool, y_sb, h2d, w_out_b, g, b, w_r2, b_r)


def _row_gather_copy(src_hbm, row, dst_ref, dst_row, sem):
    return pltpu.make_async_copy(src_hbm.at[pl.ds(row, 1)], dst_ref.at[pl.ds(dst_row, 1)], sem)


def _row_scatter_copy(src_ref, src_row, dst_hbm, row, sem):
    return pltpu.make_async_copy(src_ref.at[pl.ds(src_row, 1)], dst_hbm.at[pl.ds(row, 1)], sem)


def _dispatch_kernel(pos_ref, tail_ref, nvb_ref, h_ref, xs_hbm, sbuf, zbuf, sem, zsem):
    i = pl.program_id(0)
    nb = pl.num_programs(0)
    rows = DISPATCH_TILE
    blk = DISPATCH_ROWS
    n_blocks = xs_hbm.shape[0] // blk

    def zero_copy(start):
        return pltpu.make_async_copy(zbuf, xs_hbm.at[pl.ds(pl.multiple_of(start, blk), blk)], zsem)

    @pl.when(i == 0)
    def _():
        zbuf[...] = jnp.zeros_like(zbuf)
        for e in range(N_EXPERTS):
            @pl.when(tail_ref[e] >= 0)
            def _():
                zero_copy(tail_ref[e]).start()

        def start_unused(b, carry):
            zero_copy(b * blk).start()
            return carry

        def wait_unused(b, carry):
            zero_copy(b * blk).wait()
            return carry

        lax.fori_loop(nvb_ref[0], n_blocks, start_unused, 0)
        for e in range(N_EXPERTS):
            @pl.when(tail_ref[e] >= 0)
            def _():
                zero_copy(tail_ref[e]).wait()
        lax.fori_loop(nvb_ref[0], n_blocks, wait_unused, 0)

    slot = i % 2

    def wait_slot(s):
        for _ in range(TOP_K):
            pltpu.make_async_copy(sbuf.at[s], xs_hbm.at[pl.ds(0, rows)], sem.at[s]).wait()

    @pl.when(i >= 2)
    def _():
        wait_slot(slot)

    sbuf[slot] = h_ref[...]

    def body(r, carry):
        for k in range(TOP_K):
            _row_scatter_copy(sbuf.at[slot], r, xs_hbm, pos_ref[(i * rows + r) * TOP_K + k],
                              sem.at[slot]).start()
        return carry

    lax.fori_loop(0, rows, body, 0, unroll=8)

    @pl.when(i == nb - 1)
    def _():
        @pl.when(nb >= 2)
        def _():
            wait_slot(1 - slot)
        wait_slot(slot)


def _dispatch(pos, tail, n_valid_blocks, h1, n_blocks):
    T, D = h1.shape
    rows = DISPATCH_TILE
    grid_spec = pltpu.PrefetchScalarGridSpec(
        num_scalar_prefetch=3,
        grid=(T // rows,),
        in_specs=[pl.BlockSpec((rows, D), lambda i, pos, tail, nvb: (i, 0))],
        out_specs=pl.BlockSpec(memory_space=pl.ANY),
        scratch_shapes=[pltpu.VMEM((2, rows, D), F32), pltpu.VMEM((DISPATCH_ROWS, D), F32),
                        pltpu.SemaphoreType.DMA((2,)), pltpu.SemaphoreType.DMA],
    )
    return pl.pallas_call(
        _dispatch_kernel,
        grid_spec=grid_spec,
        out_shape=jax.ShapeDtypeStruct((n_blocks * DISPATCH_ROWS, D), F32),
        compiler_params=_params(("arbitrary",)),
        name="dispatch",
    )(pos, tail, n_valid_blocks, h1)


def _expert_kernel(be_ref, nvb_ref, xs_ref, wg_ref, wu_ref, wd_ref, ys_ref, wgb, wub, wdb):
    i = pl.program_id(0)
    e = be_ref[i]
    prev = be_ref[jnp.maximum(i - 1, 0)]

    @pl.when(jnp.logical_or(i == 0, e != prev))
    def _():
        wgb[...] = wg_ref[...].astype(BF16)
        wub[...] = wu_ref[...].astype(BF16)
        wdb[...] = wd_ref[...].astype(BF16)

    @pl.when(i < nvb_ref[0])
    def _():
        x = xs_ref[...].astype(BF16)
        gate = jnp.dot(x, wgb[...], preferred_element_type=F32)
        up = jnp.dot(x, wub[...], preferred_element_type=F32)
        hid = (gate * jax.nn.sigmoid(gate) * up).astype(BF16)
        ys_ref[...] = jnp.dot(hid, wdb[...], preferred_element_type=F32)

    @pl.when(i >= nvb_ref[0])
    def _():
        ys_ref[...] = jnp.zeros_like(ys_ref)


def _experts(layer, block_e, n_valid_blocks, xs, w_gate, w_up, w_down):
    P, D = xs.shape
    DE = w_gate.shape[-1]
    rows = DISPATCH_ROWS
    n_blocks = P // rows
    x_map = lambda i, be, nvb: (jnp.minimum(i, jnp.maximum(nvb[0] - 1, 0)), 0)
    grid_spec = pltpu.PrefetchScalarGridSpec(
        num_scalar_prefetch=2,
        grid=(n_blocks,),
        in_specs=[pl.BlockSpec((rows, D), x_map),
                  pl.BlockSpec((None, None, D, DE), lambda i, be, nvb: (layer, be[i], 0, 0)),
                  pl.BlockSpec((None, None, D, DE), lambda i, be, nvb: (layer, be[i], 0, 0)),
                  pl.BlockSpec((None, None, DE, D), lambda i, be, nvb: (layer, be[i], 0, 0))],
        out_specs=pl.BlockSpec((rows, D), lambda i, be, nvb: (i, 0)),
        scratch_shapes=[pltpu.VMEM((D, DE), BF16), pltpu.VMEM((D, DE), BF16),
                        pltpu.VMEM((DE, D), BF16)],
    )
    return pl.pallas_call(
        _expert_kernel,
        grid_spec=grid_spec,
        out_shape=jax.ShapeDtypeStruct((P, D), F32),
        compiler_params=_params(("arbitrary",)),
        name="experts",
    )(block_e, n_valid_blocks, xs, w_gate, w_up, w_down)


def _combine_kernel(alpha, with_proj, pos_ref, ys_hbm, h_ref, route_ref, g_ref, b_ref, *refs):
    if with_proj:
        w_refs, o_ref, out_refs = refs[0:4], refs[4], refs[5:9]
        buf_a, buf_b, sem, fence_sem, xb_ref, xp_ref = refs[9:]
    else:
        o_ref, buf_a, buf_b, sem, fence_sem = refs
    i = pl.program_id(0)
    nb = pl.num_programs(0)
    tile = o_ref.shape[0]
    half = tile // 2
    bufs = (buf_a, buf_b)

    def issue(tile_idx, part, lo, hi):
        base = tile_idx * tile + part * half
        for r in range(lo, hi):
            for k in range(TOP_K):
                _row_gather_copy(ys_hbm, pos_ref[(base + r) * TOP_K + k], bufs[part].at[k], r,
                                 sem.at[part]).start()

    def wait(part):
        for k in range(TOP_K):
            pltpu.make_async_copy(ys_hbm.at[pl.ds(0, half)], bufs[part].at[k],
                                  sem.at[part]).wait()

    def finish(part):
        rows = slice(part * half, (part + 1) * half)
        route = route_ref[rows, :]
        moe = route[:, 2:3] * bufs[part][0] + route[:, 3:4] * bufs[part][1]
        o_ref[rows, :] = _layer_norm(alpha * h_ref[rows, :] + moe, g_ref[...], b_ref[...])

    def fence():
        pl.semaphore_signal(fence_sem, 1)
        pl.semaphore_wait(fence_sem, 1)

    @pl.when(i == 0)
    def _():
        issue(0, 0, 0, half)
        issue(0, 1, 0, half)

    nxt = jnp.minimum(i + 1, nb - 1)
    wait(0)
    wait(1)
    finish(0)
    fence()
    if with_proj:
        stages = _projection_stages(o_ref, w_refs, out_refs, xb_ref, xp_ref)
        issue(nxt, 0, 0, half // 2)
        finish(1)
        todo = [(0, r) for r in range(half // 2, half)] + [(1, r) for r in range(half)]
        per_stage = -(-len(todo) // len(stages))
        for n, stage in enumerate(stages):
            fence()
            for part, r in todo[n * per_stage:(n + 1) * per_stage]:
                issue(nxt, part, r, r + 1)
            stage()
    else:
        issue(nxt, 0, 0, half)
        finish(1)
        fence()
        issue(nxt, 1, 0, half)

    @pl.when(i == nb - 1)
    def _():
        wait(0)
        wait(1)


def _combine_norm(alpha, pos, ys, h1, route, g, b, proj_weights=None):
    T, D = h1.shape
    rows = ROW_TILE
    with_proj = proj_weights is not None
    const = lambda a: pl.BlockSpec(a.shape, lambda i, pos: (0,) * a.ndim)
    in_specs = [pl.BlockSpec(memory_space=pl.ANY),
                pl.BlockSpec((rows, D), lambda i, pos: (i, 0)),
                pl.BlockSpec((rows, LANES), lambda i, pos: (i, 0)),
                const(g), const(b)]
    out_specs = [pl.BlockSpec((rows, D), lambda i, pos: (i, 0))]
    out_shape = [jax.ShapeDtypeStruct((T, D), F32)]
    args = [pos, ys, h1, route, g, b]
    if with_proj:
        width = proj_weights[0].shape[1]
        in_specs += [const(w) for w in proj_weights]
        args += list(proj_weights)
        row_major = lambda: pl.BlockSpec((rows, width), lambda i, pos: (i, 0))
        feat_major = lambda: pl.BlockSpec((width, rows), lambda i, pos: (0, i))
        out_specs += [row_major(), feat_major(), row_major(), feat_major()]
        out_shape += [jax.ShapeDtypeStruct((T, width), F32), jax.ShapeDtypeStruct((width, T), BF16),
                      jax.ShapeDtypeStruct((T, width), BF16), jax.ShapeDtypeStruct((width, T), BF16)]
    grid_spec = pltpu.PrefetchScalarGridSpec(
        num_scalar_prefetch=1,
        grid=(T // rows,),
        in_specs=in_specs,
        out_specs=out_specs,
        scratch_shapes=[pltpu.VMEM((TOP_K, rows // 2, D), F32), pltpu.VMEM((TOP_K, rows // 2, D), F32),
                        pltpu.SemaphoreType.DMA((2,)), pltpu.SemaphoreType.REGULAR] +
                       ([pltpu.VMEM((rows, D), BF16), pltpu.VMEM((rows, D), BF16)] if with_proj else []),
    )
    out = pl.pallas_call(
        functools.partial(_combine_kernel, alpha, with_proj),
        grid_spec=grid_spec,
        out_shape=out_shape,
        compiler_params=_params(("arbitrary",)),
        name="combine_norm_proj" if with_proj else "combine_norm",
    )(*args)
    return out if with_proj else out[0]


def _dispatch_plan(route, counts_row, n_blocks):
    rows = DISPATCH_ROWS
    experts = jnp.arange(N_EXPERTS, dtype=jnp.int32)
    counts = counts_row[0, N_EXPERT_GROUPS:N_EXPERT_GROUPS + N_EXPERTS].astype(jnp.int32)
    padded = ((counts + rows - 1) // rows) * rows
    pend = jnp.cumsum(padded)
    pstart = pend - padded
    block_start = jnp.arange(n_blocks, dtype=jnp.int32) * rows
    block_e = jnp.minimum(jnp.sum((block_start[:, None] >= pend[None, :]).astype(jnp.int32), axis=1),
                          N_EXPERTS - 1).astype(jnp.int32)
    n_valid_blocks = (pend[-1:] // rows).astype(jnp.int32)
    tail = jnp.where(counts > 0, pend - rows, -1).astype(jnp.int32)
    e = route[:, :TOP_K].astype(jnp.int32)
    rank = route[:, 4:4 + TOP_K].astype(jnp.int32)
    base = jnp.sum(jnp.where(e[:, :, None] == experts[None, None, :], pstart[None, None, :], 0),
                   axis=-1)
    pos = (base + rank).reshape(-1).astype(jnp.int32)
    return block_e, n_valid_blocks, tail, pos


def kernel(x, w_in, w_pool, pool_scale, w_out, ln1_g, ln1_b, w_router_group, b_router_group,
           w_router_expert, b_router_expert, w_gate, w_up, w_down, ln2_g, ln2_b):
    B, S, D = x.shape
    depth = w_in.shape[0]
    T = B * S
    alpha = (2.0 * depth) ** 0.25
    pool_width = w_pool.shape[1] * w_pool.shape[2]
    sb_width = (w_in.shape[2] - pool_width) // 3
    n_blocks = -(-T * TOP_K // DISPATCH_ROWS) + N_EXPERTS

    col_scale = jnp.concatenate([jnp.ones((pool_width,), F32),
                                 jnp.full((sb_width,), -(SB_HEAD_DIM ** -0.5), F32),
                                 jnp.ones((2 * sb_width,), F32)])
    w_in_s = w_in * col_scale
    w_u = w_in_s[:, :, :pool_width].astype(BF16)
    w_qt = jnp.swapaxes(w_in_s[:, :, pool_width:pool_width + sb_width], 1, 2).astype(BF16)
    w_k = w_in_s[:, :, pool_width + sb_width:pool_width + 2 * sb_width].astype(BF16)
    w_vt = jnp.swapaxes(w_in_s[:, :, pool_width + 2 * sb_width:], 1, 2).astype(BF16)
    w_pool_b = w_pool.astype(BF16)
    w_out_b = w_out.astype(BF16)
    w_r = jnp.concatenate([w_router_group, w_router_expert,
                           jnp.zeros((depth, D, LANES - N_EXPERT_GROUPS - N_EXPERTS), F32)], axis=-1)
    wr_hi = w_r.astype(BF16)
    wr_lo = (w_r - wr_hi.astype(F32)).astype(BF16)
    w_r2 = jnp.concatenate([wr_hi, wr_lo], axis=-1)
    b_r = jnp.concatenate([b_router_group.astype(F32),
                           b_router_expert.astype(F32).reshape(depth, N_EXPERTS),
                           jnp.zeros((depth, LANES - N_EXPERT_GROUPS - N_EXPERTS), F32)], axis=-1)

    h = x.reshape(T, D)
    u, q_t, k_perm, v_t = _project(h, w_u[0], w_qt[0], w_k[0], w_vt[0])
    for l in range(depth):
        y_pool = _pool(u, w_pool_b[l], pool_scale[l][None, :], B, S)
        y_sb = _attention(q_t, k_perm, v_t, B, S)
        h1, route, counts = _mix_norm_route(alpha, y_pool, y_sb, h, w_out_b[l], ln1_g[l][None, :],
                                            ln1_b[l][None, :], w_r2[l], b_r[l][None, :])
        block_e, n_valid_blocks, tail, pos = _dispatch_plan(route, counts, n_blocks)
        xs = _dispatch(pos, tail, n_valid_blocks, h1, n_blocks)
        ys = _experts(l, block_e, n_valid_blocks, xs, w_gate, w_up, w_down)
        nxt = None if l + 1 == depth else (w_u[l + 1], w_qt[l + 1], w_k[l + 1], w_vt[l + 1])
        out = _combine_norm(alpha, pos, ys, h1, route, ln2_g[l][None, :], ln2_b[l][None, :], nxt)
        if nxt is None:
            h = out
        else:
            h, u, q_t, k_perm, v_t = out
    return h.reshape(B, S, D)
```

```python
import functools

import jax
import jax.numpy as jnp
from jax import lax
from jax.experimental import pallas as pl
from jax.experimental.pallas import tpu as pltpu

F32 = jnp.float32
BF16 = jnp.bfloat16

POOL_WINDOWS = (2, 4, 8, 16)
SB_HEAD_DIM = 64
N_EXPERT_GROUPS = 4
EXPERTS_PER_GROUP = 8
N_EXPERTS = N_EXPERT_GROUPS * EXPERTS_PER_GROUP
TOP_K = 2
LN_EPS = 1e-5

LANES = 128
SUBLANES = 8
ROW_TILE = 512
QUERY_BLOCK = 256
KEY_BLOCK = 128
KEY_SEG = KEY_BLOCK // SUBLANES
PAIRS_PER_STEP = 4
LOOP_PAIRS = 2
DISPATCH_ROWS = 256
DISPATCH_TILE = 256
NEG_BIG = -1e30
VMEM_LIMIT = 48 * 1024 * 1024


def _params(sem):
    return pltpu.CompilerParams(dimension_semantics=sem, vmem_limit_bytes=VMEM_LIMIT)


def _proj_kernel(x_ref, wu_ref, wqt_ref, wk_ref, wvt_ref, u_ref, qt_ref, k_ref, vt_ref,
                 xb_ref, xp_ref):
    for stage in _projection_stages(x_ref, (wu_ref, wqt_ref, wk_ref, wvt_ref),
                                    (u_ref, qt_ref, k_ref, vt_ref), xb_ref, xp_ref):
        stage()


def _projection_stages(x_ref, w_refs, out_refs, xb_ref, xp_ref):
    wu_ref, wqt_ref, wk_ref, wvt_ref = w_refs
    u_ref, qt_ref, k_ref, vt_ref = out_refs
    rows = x_ref.shape[0]
    nt = (((1,), (1,)), ((), ()))

    def pool_input():
        xb_ref[...] = x_ref[...].astype(BF16)
        u_ref[...] = jnp.dot(xb_ref[...], wu_ref[...], preferred_element_type=F32)

    def queries():
        qt_ref[...] = lax.dot_general(wqt_ref[...], xb_ref[...], nt,
                                      preferred_element_type=F32).astype(qt_ref.dtype)

    def keys():
        p_i = lax.broadcasted_iota(jnp.int32, (KEY_BLOCK, KEY_BLOCK), 0)
        t_i = lax.broadcasted_iota(jnp.int32, (KEY_BLOCK, KEY_BLOCK), 1)
        src = (p_i % SUBLANES) * KEY_SEG + p_i // SUBLANES
        perm = jnp.where(t_i == src, 1.0, 0.0).astype(BF16)
        for g in range(rows // KEY_BLOCK):
            blk = slice(g * KEY_BLOCK, (g + 1) * KEY_BLOCK)
            xp_ref[blk, :] = jnp.dot(perm, xb_ref[blk, :],
                                     preferred_element_type=F32).astype(BF16)
        k_ref[...] = jnp.dot(xp_ref[...], wk_ref[...],
                             preferred_element_type=F32).astype(k_ref.dtype)

    def values():
        vt_ref[...] = lax.dot_general(wvt_ref[...], xp_ref[...], nt,
                                      preferred_element_type=F32).astype(vt_ref.dtype)

    return pool_input, queries, keys, values


def _project(h2d, w_u, w_qt, w_k, w_vt):
    T, D = h2d.shape
    width = w_u.shape[1]
    row_major = lambda: pl.BlockSpec((ROW_TILE, width), lambda i: (i, 0))
    feat_major = lambda: pl.BlockSpec((width, ROW_TILE), lambda i: (0, i))
    full = lambda a: pl.BlockSpec(a.shape, lambda i: (0, 0))
    return pl.pallas_call(
        _proj_kernel,
        grid=(T // ROW_TILE,),
        in_specs=[pl.BlockSpec((ROW_TILE, D), lambda i: (i, 0)),
                  full(w_u), full(w_qt), full(w_k), full(w_vt)],
        out_specs=[row_major(), feat_major(), row_major(), feat_major()],
        out_shape=[jax.ShapeDtypeStruct((T, width), F32),
                   jax.ShapeDtypeStruct((width, T), BF16),
                   jax.ShapeDtypeStruct((T, width), BF16),
                   jax.ShapeDtypeStruct((width, T), BF16)],
        scratch_shapes=[pltpu.VMEM((ROW_TILE, D), BF16), pltpu.VMEM((ROW_TILE, D), BF16)],
        compiler_params=_params(("parallel",)),
        name="proj",
    )(h2d, w_u, w_qt, w_k, w_vt)


def _pool_kernel(u_ref, w_ref, s_ref, o_ref):
    S = u_ref.shape[0]
    row = lax.broadcasted_iota(jnp.int32, (S, LANES), 0)
    for g, win in enumerate(POOL_WINDOWS):
        u = u_ref[:, g * LANES:(g + 1) * LANES]
        acc = u
        k = 1
        while k < win:
            shifted = jnp.where(row >= k, pltpu.roll(acc, k, 0), 0.0)
            acc = acc + shifted
            k *= 2
        count = jnp.minimum(row + 1, win).astype(F32)
        pooled = acc / count - u
        mixed = jnp.dot(pooled.astype(BF16), w_ref[g], preferred_element_type=F32)
        o_ref[:, g * LANES:(g + 1) * LANES] = (
            mixed * s_ref[:, g * LANES:(g + 1) * LANES]).astype(o_ref.dtype)


def _pool(u, w_pool_b, pool_scale, B, S):
    T, W = u.shape
    return pl.pallas_call(
        _pool_kernel,
        grid=(B,),
        in_specs=[pl.BlockSpec((S, W), lambda b: (b, 0)),
                  pl.BlockSpec(w_pool_b.shape, lambda b: (0, 0, 0)),
                  pl.BlockSpec((1, W), lambda b: (0, 0))],
        out_specs=pl.BlockSpec((S, W), lambda b: (b, 0)),
        out_shape=jax.ShapeDtypeStruct((T, W), BF16),
        compiler_params=_params(("parallel",)),
        name="pool",
    )(u, w_pool_b, pool_scale)


def _attn_kernel(qt_ref, k_ref, vt_ref, o_ref, acc_ref, car_ref, z0_ref, z1_ref, a0_ref, a1_ref):
    i = pl.program_id(2)
    qb = QUERY_BLOCK
    kb = KEY_BLOCK
    width = 2 * qb
    pairs = range(PAIRS_PER_STEP)
    pair_rows = [slice(p * LANES, (p + 1) * LANES) for p in pairs]
    feat = lax.broadcasted_iota(jnp.int32, (LANES, qb), 0)
    rhs = []
    for p in pairs:
        q2 = qt_ref[pair_rows[p], :]
        zero = jnp.zeros_like(q2)
        rhs.append(jnp.concatenate([jnp.where(feat < SB_HEAD_DIM, q2, zero),
                                    jnp.where(feat >= SB_HEAD_DIM, q2, zero)], axis=1))
    sub = lax.broadcasted_iota(jnp.int32, (SUBLANES, width), 0)
    lane = lax.broadcasted_iota(jnp.int32, (SUBLANES, width), 1)
    q_pos = i * qb + jnp.bitwise_and(lane, qb - 1)

    acc_ref[...] = jnp.zeros_like(acc_ref)
    car_ref[...] = jnp.ones_like(car_ref)

    n_diag = qb // kb
    last = (i + 1) * n_diag - 1

    def block_start(n):
        return pl.multiple_of(jnp.clip(last - n, 0, last) * kb, kb)

    def scores(n, z_ref, ps=pairs):
        for p in ps:
            z_ref[p] = jnp.dot(k_ref[pl.ds(block_start(n), kb), pair_rows[p]], rhs[p],
                               preferred_element_type=F32)

    def values(n, a_ref, ps=pairs):
        for p in ps:
            acc_ref[p] += jnp.dot(vt_ref[pair_rows[p], pl.ds(block_start(n), kb)], a_ref[p],
                                  preferred_element_type=F32)

    def weights(n, z_ref, a_ref, masked, ps=pairs):
        for p in ps:
            weights_one(n, z_ref.at[p], a_ref.at[p], car_ref.at[p], masked)

    def weights_one(n, z_ref, a_ref, car_ref, masked):
        start = block_start(n)
        run = jnp.ones((SUBLANES, width), F32)
        for v in reversed(range(KEY_SEG)):
            slab = slice(v * SUBLANES, (v + 1) * SUBLANES)
            beta = 1.0 / (1.0 + jnp.exp(z_ref[slab, :]))
            rest = 1.0 - beta
            if masked:
                m = (start + sub * KEY_SEG + v) < q_pos
                beta = jnp.where(m, beta, 0.0)
                rest = jnp.where(m, rest, 1.0)
            z_ref[slab, :] = beta * run
            run = run * rest
        incl = run
        for step in (1, 2, 4):
            shifted = pltpu.roll(incl, SUBLANES - step, 0)
            incl = incl * jnp.where(sub < SUBLANES - step, shifted, 1.0)
        later = jnp.where(sub < SUBLANES - 1, pltpu.roll(incl, SUBLANES - 1, 0), 1.0)
        base = later * car_ref[...]
        a_ref[...] = jnp.concatenate(
            [z_ref[v * SUBLANES:(v + 1) * SUBLANES, :] * base for v in range(KEY_SEG)],
            axis=0).astype(BF16)
        car_ref[...] *= jnp.broadcast_to(incl[0:1, :], (SUBLANES, width))

    assert n_diag == 2
    scores(0, z0_ref)
    scores(1, z1_ref)
    weights(0, z0_ref, a0_ref, True)
    scores(2, z0_ref)
    weights(1, z1_ref, a1_ref, True)
    values(0, a0_ref)

    for g in range(0, PAIRS_PER_STEP, LOOP_PAIRS):
        ps = tuple(range(g, g + LOOP_PAIRS))

        def two_blocks(t, carry, ps=ps):
            n = 2 + 2 * t
            values(n - 1, a1_ref, ps)
            weights(n, z0_ref, a0_ref, False, ps)
            scores(n + 1, z1_ref, ps)
            values(n, a0_ref, ps)
            weights(n + 1, z1_ref, a1_ref, False, ps)
            scores(n + 2, z0_ref, ps)
            return carry

        lax.fori_loop(0, i, two_blocks, 0)
    values(last, a1_ref)

    for p in pairs:
        acc = acc_ref[p]
        both = jnp.where(feat < SB_HEAD_DIM, acc[:, :qb], acc[:, qb:])
        o_ref[:, pair_rows[p]] = both.T.astype(o_ref.dtype)


def _attention(q_t, k_perm, v_t, B, S):
    W, T = q_t.shape
    qb = QUERY_BLOCK
    nq = S // qb
    step_w = PAIRS_PER_STEP * LANES
    groups = W // step_w
    per_pair = lambda shape, dtype: pltpu.VMEM((PAIRS_PER_STEP,) + shape, dtype)
    return pl.pallas_call(
        _attn_kernel,
        grid=(B, groups, nq),
        in_specs=[pl.BlockSpec((step_w, qb), lambda b, g, i: (g, b * nq + i)),
                  pl.BlockSpec((S, step_w), lambda b, g, i: (b, g)),
                  pl.BlockSpec((step_w, S), lambda b, g, i: (g, b))],
        out_specs=pl.BlockSpec((qb, step_w), lambda b, g, i: (b * nq + i, g)),
        out_shape=jax.ShapeDtypeStruct((T, W), BF16),
        scratch_shapes=[per_pair((LANES, 2 * qb), F32), per_pair((SUBLANES, 2 * qb), F32),
                        per_pair((KEY_BLOCK, 2 * qb), F32), per_pair((KEY_BLOCK, 2 * qb), F32),
                        per_pair((KEY_BLOCK, 2 * qb), BF16), per_pair((KEY_BLOCK, 2 * qb), BF16)],
        compiler_params=_params(("parallel", "parallel", "parallel")),
        name="attn",
    )(q_t, k_perm, v_t)


def _layer_norm(y, g, b):
    mu = jnp.mean(y, axis=-1, keepdims=True)
    yc = y - mu
    var = jnp.mean(yc * yc, axis=-1, keepdims=True)
    return yc * lax.rsqrt(var + LN_EPS) * g + b


def _mix_kernel(alpha, yp_ref, ys_ref, h_ref, wo_ref, g_ref, b_ref, wr_ref, br_ref,
                h1_ref, route_ref, cnt_ref, run_ref):
    @pl.when(pl.program_id(0) == 0)
    def _():
        run_ref[...] = jnp.zeros_like(run_ref)

    half = yp_ref.shape[1]
    mix = (jnp.dot(yp_ref[...], wo_ref[:half, :], preferred_element_type=F32) +
           jnp.dot(ys_ref[...], wo_ref[half:, :], preferred_element_type=F32))
    h1 = _layer_norm(alpha * h_ref[...] + mix, g_ref[...], b_ref[...])
    h1_ref[...] = h1

    hh = h1.astype(BF16)
    hl = (h1 - hh.astype(F32)).astype(BF16)
    prod_h = jnp.dot(hh, wr_ref[...], preferred_element_type=F32)
    prod_l = jnp.dot(hl, wr_ref[:, :LANES], preferred_element_type=F32)
    logits = prod_h[:, :LANES] + prod_l + prod_h[:, LANES:] + br_ref[...]
    col = lax.broadcasted_iota(jnp.int32, logits.shape, 1)
    gl = jnp.where(col < N_EXPERT_GROUPS, logits, NEG_BIG)
    gmax = jnp.max(gl, axis=-1, keepdims=True)
    gidx = jnp.min(jnp.where(gl == gmax, col, LANES), axis=-1, keepdims=True)
    g_p = 1.0 / jnp.sum(jnp.exp(gl - gmax), axis=-1, keepdims=True)
    lo_col = N_EXPERT_GROUPS + EXPERTS_PER_GROUP * gidx
    el = jnp.where(col >= lo_col, jnp.where(col < lo_col + EXPERTS_PER_GROUP, logits, NEG_BIG),
                   NEG_BIG)
    m1 = jnp.max(el, axis=-1, keepdims=True)
    i1 = jnp.min(jnp.where(el == m1, col, LANES), axis=-1, keepdims=True)
    el2 = jnp.where(col == i1, NEG_BIG, el)
    m2 = jnp.max(el2, axis=-1, keepdims=True)
    i2 = jnp.min(jnp.where(el2 == m2, col, LANES), axis=-1, keepdims=True)
    ratio = jnp.exp(m2 - m1)
    gate1 = g_p / (1.0 + ratio)
    gate2 = g_p * ratio / (1.0 + ratio)
    e1 = (i1 - N_EXPERT_GROUPS).astype(F32)
    e2 = (i2 - N_EXPERT_GROUPS).astype(F32)

    rows = logits.shape[0]
    hit1 = col == i1
    hit2 = col == i2
    onehot = jnp.where(hit1, 1.0, jnp.where(hit2, 1.0, 0.0))
    r_i = lax.broadcasted_iota(jnp.int32, (rows, rows), 0)
    c_i = lax.broadcasted_iota(jnp.int32, (rows, rows), 1)
    earlier = jnp.where(c_i < r_i, 1.0, 0.0).astype(BF16)
    before = run_ref[...] + jnp.dot(earlier, onehot.astype(BF16), preferred_element_type=F32)
    rank1 = jnp.sum(jnp.where(hit1, before, 0.0), axis=-1, keepdims=True)
    rank2 = jnp.sum(jnp.where(hit2, before, 0.0), axis=-1, keepdims=True)
    run_ref[...] += jnp.sum(onehot, axis=0, keepdims=True)
    cnt_ref[...] = run_ref[...]

    route_ref[...] = jnp.where(col == 0, e1, jnp.where(col == 1, e2,
                               jnp.where(col == 2, gate1, jnp.where(col == 3, gate2,
                               jnp.where(col == 4, rank1, jnp.where(col == 5, rank2, 0.0))))))


def _mix_norm_route(alpha, y_pool, y_sb, h2d, w_out_b, g, b, w_r2, b_r):
    T, D = h2d.shape
    half = y_pool.shape[1]
    full = lambda a: pl.BlockSpec(a.shape, lambda i: (0,) * a.ndim)
    return pl.pallas_call(
        functools.partial(_mix_kernel, alpha),
        grid=(T // ROW_TILE,),
        in_specs=[pl.BlockSpec((ROW_TILE, half), lambda i: (i, 0)),
                  pl.BlockSpec((ROW_TILE, half), lambda i: (i, 0)),
                  pl.BlockSpec((ROW_TILE, D), lambda i: (i, 0)),
                  full(w_out_b), full(g), full(b), full(w_r2), full(b_r)],
        out_specs=[pl.BlockSpec((ROW_TILE, D), lambda i: (i, 0)),
                   pl.BlockSpec((ROW_TILE, LANES), lambda i: (i, 0)),
                   pl.BlockSpec((1, LANES), lambda i: (0, 0))],
        out_shape=[jax.ShapeDtypeStruct((T, D), F32), jax.ShapeDtypeStruct((T, LANES), F32),
                   jax.ShapeDtypeStruct((1, LANES), F32)],
        scratch_shapes=[pltpu.VMEM((1, LANES), F32)],
        compiler_params=_params(("arbitrary",)),
        name="mix_norm_route",
    )(y_pool, y_sb, h2d, w_out_b, g, b, w_r2, b_r)


def _row_gather_copy(src_hbm, row, dst_ref, dst_row, sem):
    return pltpu.make_async_copy(src_hbm.at[pl.ds(row, 1)], dst_ref.at[pl.ds(dst_row, 1)], sem)


def _row_scatter_copy(src_ref, src_row, dst_hbm, row, sem):
    return pltpu.make_async_copy(src_ref.at[pl.ds(src_row, 1)], dst_hbm.at[pl.ds(row, 1)], sem)


def _dispatch_kernel(pos_ref, tail_ref, nvb_ref, h_ref, xs_hbm, sbuf, zbuf, sem, zsem):
    i = pl.program_id(0)
    nb = pl.num_programs(0)
    rows = DISPATCH_TILE
    blk = DISPATCH_ROWS
    n_blocks = xs_hbm.shape[0] // blk

    def zero_copy(start):
        return pltpu.make_async_copy(zbuf, xs_hbm.at[pl.ds(pl.multiple_of(start, blk), blk)], zsem)

    @pl.when(i == 0)
    def _():
        zbuf[...] = jnp.zeros_like(zbuf)
        for e in range(N_EXPERTS):
            @pl.when(tail_ref[e] >= 0)
            def _():
                zero_copy(tail_ref[e]).start()

        def start_unused(b, carry):
            zero_copy(b * blk).start()
            return carry

        def wait_unused(b, carry):
            zero_copy(b * blk).wait()
            return carry

        lax.fori_loop(nvb_ref[0], n_blocks, start_unused, 0)
        for e in range(N_EXPERTS):
            @pl.when(tail_ref[e] >= 0)
            def _():
                zero_copy(tail_ref[e]).wait()
        lax.fori_loop(nvb_ref[0], n_blocks, wait_unused, 0)

    slot = i % 2

    def wait_slot(s):
        for _ in range(TOP_K):
            pltpu.make_async_copy(sbuf.at[s], xs_hbm.at[pl.ds(0, rows)], sem.at[s]).wait()

    @pl.when(i >= 2)
    def _():
        wait_slot(slot)

    sbuf[slot] = h_ref[...]

    def body(r, carry):
        for k in range(TOP_K):
            _row_scatter_copy(sbuf.at[slot], r, xs_hbm, pos_ref[(i * rows + r) * TOP_K + k],
                              sem.at[slot]).start(priority=k)
        return carry

    lax.fori_loop(0, rows, body, 0, unroll=8)

    @pl.when(i == nb - 1)
    def _():
        @pl.when(nb >= 2)
        def _():
            wait_slot(1 - slot)
        wait_slot(slot)


def _dispatch(pos, tail, n_valid_blocks, h1, n_blocks):
    T, D = h1.shape
    rows = DISPATCH_TILE
    grid_spec = pltpu.PrefetchScalarGridSpec(
        num_scalar_prefetch=3,
        grid=(T // rows,),
        in_specs=[pl.BlockSpec((rows, D), lambda i, pos, tail, nvb: (i, 0))],
        out_specs=pl.BlockSpec(memory_space=pl.ANY),
        scratch_shapes=[pltpu.VMEM((2, rows, D), F32), pltpu.VMEM((DISPATCH_ROWS, D), F32),
                        pltpu.SemaphoreType.DMA((2,)), pltpu.SemaphoreType.DMA],
    )
    return pl.pallas_call(
        _dispatch_kernel,
        grid_spec=grid_spec,
        out_shape=jax.ShapeDtypeStruct((n_blocks * DISPATCH_ROWS, D), F32),
        compiler_params=_params(("arbitrary",)),
        name="dispatch",
    )(pos, tail, n_valid_blocks, h1)


def _expert_kernel(be_ref, nvb_ref, xs_ref, wg_ref, wu_ref, wd_ref, ys_ref, wgb, wub, wdb):
    i = pl.program_id(0)
    e = be_ref[i]
    prev = be_ref[jnp.maximum(i - 1, 0)]

    @pl.when(jnp.logical_or(i == 0, e != prev))
    def _():
        wgb[...] = wg_ref[...].astype(BF16)
        wub[...] = wu_ref[...].astype(BF16)
        wdb[...] = wd_ref[...].astype(BF16)

    @pl.when(i < nvb_ref[0])
    def _():
        x = xs_ref[...].astype(BF16)
        gate = jnp.dot(x, wgb[...], preferred_element_type=F32)
        up = jnp.dot(x, wub[...], preferred_element_type=F32)
        hid = (gate * jax.nn.sigmoid(gate) * up).astype(BF16)
        ys_ref[...] = jnp.dot(hid, wdb[...], preferred_element_type=F32)

    @pl.when(i >= nvb_ref[0])
    def _():
        ys_ref[...] = jnp.zeros_like(ys_ref)


def _experts(layer, block_e, n_valid_blocks, xs, w_gate, w_up, w_down):
    P, D = xs.shape
    DE = w_gate.shape[-1]
    rows = DISPATCH_ROWS
    n_blocks = P // rows
    x_map = lambda i, be, nvb: (jnp.minimum(i, jnp.maximum(nvb[0] - 1, 0)), 0)
    grid_spec = pltpu.PrefetchScalarGridSpec(
        num_scalar_prefetch=2,
        grid=(n_blocks,),
        in_specs=[pl.BlockSpec((rows, D), x_map),
                  pl.BlockSpec((None, None, D, DE), lambda i, be, nvb: (layer, be[i], 0, 0)),
                  pl.BlockSpec((None, None, D, DE), lambda i, be, nvb: (layer, be[i], 0, 0)),
                  pl.BlockSpec((None, None, DE, D), lambda i, be, nvb: (layer, be[i], 0, 0))],
        out_specs=pl.BlockSpec((rows, D), lambda i, be, nvb: (i, 0)),
        scratch_shapes=[pltpu.VMEM((D, DE), BF16), pltpu.VMEM((D, DE), BF16),
                        pltpu.VMEM((DE, D), BF16)],
    )
    return pl.pallas_call(
        _expert_kernel,
        grid_spec=grid_spec,
        out_shape=jax.ShapeDtypeStruct((P, D), F32),
        compiler_params=_params(("arbitrary",)),
        name="experts",
    )(block_e, n_valid_blocks, xs, w_gate, w_up, w_down)


def _combine_kernel(alpha, with_proj, pos_ref, ys_hbm, h_ref, route_ref, g_ref, b_ref, *refs):
    if with_proj:
        w_refs, o_ref, out_refs = refs[0:4], refs[4], refs[5:9]
        buf_a, buf_b, sem, fence_sem, xb_ref, xp_ref = refs[9:]
    else:
        o_ref, buf_a, buf_b, sem, fence_sem = refs
    i = pl.program_id(0)
    nb = pl.num_programs(0)
    tile = o_ref.shape[0]
    half = tile // 2
    bufs = (buf_a, buf_b)

    def issue(tile_idx, part, lo, hi):
        base = tile_idx * tile + part * half
        for r in range(lo, hi):
            for k in range(TOP_K):
                _row_gather_copy(ys_hbm, pos_ref[(base + r) * TOP_K + k], bufs[part].at[k], r,
                                 sem.at[part]).start(priority=k)

    def wait(part):
        for k in range(TOP_K):
            pltpu.make_async_copy(ys_hbm.at[pl.ds(0, half)], bufs[part].at[k],
                                  sem.at[part]).wait()

    def finish(part):
        rows = slice(part * half, (part + 1) * half)
        route = route_ref[rows, :]
        moe = route[:, 2:3] * bufs[part][0] + route[:, 3:4] * bufs[part][1]
        o_ref[rows, :] = _layer_norm(alpha * h_ref[rows, :] + moe, g_ref[...], b_ref[...])

    def fence():
        pl.semaphore_signal(fence_sem, 1)
        pl.semaphore_wait(fence_sem, 1)

    @pl.when(i == 0)
    def _():
        issue(0, 0, 0, half)
        issue(0, 1, 0, half)

    nxt = jnp.minimum(i + 1, nb - 1)
    wait(0)
    wait(1)
    finish(0)
    fence()
    if with_proj:
        stages = _projection_stages(o_ref, w_refs, out_refs, xb_ref, xp_ref)
        issue(nxt, 0, 0, half // 2)
        finish(1)
        todo = [(0, r) for r in range(half // 2, half)] + [(1, r) for r in range(half)]
        per_stage = -(-len(todo) // len(stages))
        for n, stage in enumerate(stages):
            fence()
            for part, r in todo[n * per_stage:(n + 1) * per_stage]:
                issue(nxt, part, r, r + 1)
            stage()
    else:
        issue(nxt, 0, 0, half)
        finish(1)
        fence()
        issue(nxt, 1, 0, half)

    @pl.when(i == nb - 1)
    def _():
        wait(0)
        wait(1)


def _combine_norm(alpha, pos, ys, h1, route, g, b, proj_weights=None):
    T, D = h1.shape
    rows = ROW_TILE
    with_proj = proj_weights is not None
    const = lambda a: pl.BlockSpec(a.shape, lambda i, pos: (0,) * a.ndim)
    in_specs = [pl.BlockSpec(memory_space=pl.ANY),
                pl.BlockSpec((rows, D), lambda i, pos: (i, 0)),
                pl.BlockSpec((rows, LANES), lambda i, pos: (i, 0)),
                const(g), const(b)]
    out_specs = [pl.BlockSpec((rows, D), lambda i, pos: (i, 0))]
    out_shape = [jax.ShapeDtypeStruct((T, D), F32)]
    args = [pos, ys, h1, route, g, b]
    if with_proj:
        width = proj_weights[0].shape[1]
        in_specs += [const(w) for w in proj_weights]
        args += list(proj_weights)
        row_major = lambda: pl.BlockSpec((rows, width), lambda i, pos: (i, 0))
        feat_major = lambda: pl.BlockSpec((width, rows), lambda i, pos: (0, i))
        out_specs += [row_major(), feat_major(), row_major(), feat_major()]
        out_shape += [jax.ShapeDtypeStruct((T, width), F32), jax.ShapeDtypeStruct((width, T), BF16),
                      jax.ShapeDtypeStruct((T, width), BF16), jax.ShapeDtypeStruct((width, T), BF16)]
    grid_spec = pltpu.PrefetchScalarGridSpec(
        num_scalar_prefetch=1,
        grid=(T // rows,),
        in_specs=in_specs,
        out_specs=out_specs,
        scratch_shapes=[pltpu.VMEM((TOP_K, rows // 2, D), F32), pltpu.VMEM((TOP_K, rows // 2, D), F32),
                        pltpu.SemaphoreType.DMA((2,)), pltpu.SemaphoreType.REGULAR] +
                       ([pltpu.VMEM((rows, D), BF16), pltpu.VMEM((rows, D), BF16)] if with_proj else []),
    )
    out = pl.pallas_call(
        functools.partial(_combine_kernel, alpha, with_proj),
        grid_spec=grid_spec,
        out_shape=out_shape,
        compiler_params=_params(("arbitrary",)),
        name="combine_norm_proj" if with_proj else "combine_norm",
    )(*args)
    return out if with_proj else out[0]


def _dispatch_plan(route, counts_row, n_blocks):
    rows = DISPATCH_ROWS
    experts = jnp.arange(N_EXPERTS, dtype=jnp.int32)
    counts = counts_row[0, N_EXPERT_GROUPS:N_EXPERT_GROUPS + N_EXPERTS].astype(jnp.int32)
    padded = ((counts + rows - 1) // rows) * rows
    pend = jnp.cumsum(padded)
    pstart = pend - padded
    block_start = jnp.arange(n_blocks, dtype=jnp.int32) * rows
    block_e = jnp.minimum(jnp.sum((block_start[:, None] >= pend[None, :]).astype(jnp.int32), axis=1),
                          N_EXPERTS - 1).astype(jnp.int32)
    n_valid_blocks = (pend[-1:] // rows).astype(jnp.int32)
    tail = jnp.where(counts > 0, pend - rows, -1).astype(jnp.int32)
    e = route[:, :TOP_K].astype(jnp.int32)
    rank = route[:, 4:4 + TOP_K].astype(jnp.int32)
    base = jnp.sum(jnp.where(e[:, :, None] == experts[None, None, :], pstart[None, None, :], 0),
                   axis=-1)
    pos = (base + rank).reshape(-1).astype(jnp.int32)
    return block_e, n_valid_blocks, tail, pos


def kernel(x, w_in, w_pool, pool_scale, w_out, ln1_g, ln1_b, w_router_group, b_router_group,
           w_router_expert, b_router_expert, w_gate, w_up, w_down, ln2_g, ln2_b):
    B, S, D = x.shape
    depth = w_in.shape[0]
    T = B * S
    alpha = (2.0 * depth) ** 0.25
    pool_width = w_pool.shape[1] * w_pool.shape[2]
    sb_width = (w_in.shape[2] - pool_width) // 3
    n_blocks = -(-T * TOP_K // DISPATCH_ROWS) + N_EXPERTS

    col_scale = jnp.concatenate([jnp.ones((pool_width,), F32),
                                 jnp.full((sb_width,), -(SB_HEAD_DIM ** -0.5), F32),
                                 jnp.ones((2 * sb_width,), F32)])
    w_in_s = w_in * col_scale
    w_u = w_in_s[:, :, :pool_width].astype(BF16)
    w_qt = jnp.swapaxes(w_in_s[:, :, pool_width:pool_width + sb_width], 1, 2).astype(BF16)
    w_k = w_in_s[:, :, pool_width + sb_width:pool_width + 2 * sb_width].astype(BF16)
    w_vt = jnp.swapaxes(w_in_s[:, :, pool_width + 2 * sb_width:], 1, 2).astype(BF16)
    w_pool_b = w_pool.astype(BF16)
    w_out_b = w_out.astype(BF16)
    w_r = jnp.concatenate([w_router_group, w_router_expert,
                           jnp.zeros((depth, D, LANES - N_EXPERT_GROUPS - N_EXPERTS), F32)], axis=-1)
    wr_hi = w_r.astype(BF16)
    wr_lo = (w_r - wr_hi.astype(F32)).astype(BF16)
    w_r2 = jnp.concatenate([wr_hi, wr_lo], axis=-1)
    b_r = jnp.concatenate([b_router_group.astype(F32),
                           b_router_expert.astype(F32).reshape(depth, N_EXPERTS),
                           jnp.zeros((depth, LANES - N_EXPERT_GROUPS - N_EXPERTS), F32)], axis=-1)

    h = x.reshape(T, D)
    u, q_t, k_perm, v_t = _project(h, w_u[0], w_qt[0], w_k[0], w_vt[0])
    for l in range(depth):
        y_pool = _pool(u, w_pool_b[l], pool_scale[l][None, :], B, S)
        y_sb = _attention(q_t, k_perm, v_t, B, S)
        h1, route, counts = _mix_norm_route(alpha, y_pool, y_sb, h, w_out_b[l], ln1_g[l][None, :],
                                            ln1_b[l][None, :], w_r2[l], b_r[l][None, :])
        block_e, n_valid_blocks, tail, pos = _dispatch_plan(route, counts, n_blocks)
        xs = _dispatch(pos, tail, n_valid_blocks, h1, n_blocks)
        ys = _experts(l, block_e, n_valid_blocks, xs, w_gate, w_up, w_down)
        nxt = None if l + 1 == depth else (w_u[l + 1], w_qt[l + 1], w_k[l + 1], w_vt[l + 1])
        out = _combine_norm(alpha, pos, ys, h1, route, ln2_g[l][None, :], ln2_b[l][None, :], nxt)
        if nxt is None:
            h = out
        else:
            h, u, q_t, k_perm, v_t = out
    return h.reshape(B, S, D)
```

```python
import functools

import jax
import jax.numpy as jnp
from jax import lax
from jax.experimental import pallas as pl
from jax.experimental.pallas import tpu as pltpu

F32 = jnp.float32
BF16 = jnp.bfloat16

POOL_WINDOWS = (2, 4, 8, 16)
SB_HEAD_DIM = 64
N_EXPERT_GROUPS = 4
EXPERTS_PER_GROUP = 8
N_EXPERTS = N_EXPERT_GROUPS * EXPERTS_PER_GROUP
TOP_K = 2
LN_EPS = 1e-5

LANES = 128
SUBLANES = 8
ROW_TILE = 512
QUERY_BLOCK = 256
KEY_BLOCK = 128
KEY_SEG = KEY_BLOCK // SUBLANES
PAIRS_PER_STEP = 4
LOOP_PAIRS = 2
DISPATCH_ROWS = 256
DISPATCH_TILE = 256
NEG_BIG = -1e30
VMEM_LIMIT = 48 * 1024 * 1024


def _params(sem):
    return pltpu.CompilerParams(dimension_semantics=sem, vmem_limit_bytes=VMEM_LIMIT)


def _proj_kernel(x_ref, wu_ref, wqt_ref, wk_ref, wvt_ref, u_ref, qt_ref, k_ref, vt_ref,
                 xb_ref, xp_ref):
    for stage in _projection_stages(x_ref, (wu_ref, wqt_ref, wk_ref, wvt_ref),
                                    (u_ref, qt_ref, k_ref, vt_ref), xb_ref, xp_ref):
        stage()


def _projection_stages(x_ref, w_refs, out_refs, xb_ref, xp_ref):
    wu_ref, wqt_ref, wk_ref, wvt_ref = w_refs
    u_ref, qt_ref, k_ref, vt_ref = out_refs
    rows = x_ref.shape[0]
    nt = (((1,), (1,)), ((), ()))

    def pool_input():
        xb_ref[...] = x_ref[...].astype(BF16)
        u_ref[...] = jnp.dot(xb_ref[...], wu_ref[...], preferred_element_type=F32)

    def queries():
        qt_ref[...] = lax.dot_general(wqt_ref[...], xb_ref[...], nt,
                                      preferred_element_type=F32).astype(qt_ref.dtype)

    def keys():
        p_i = lax.broadcasted_iota(jnp.int32, (KEY_BLOCK, KEY_BLOCK), 0)
        t_i = lax.broadcasted_iota(jnp.int32, (KEY_BLOCK, KEY_BLOCK), 1)
        src = (p_i % SUBLANES) * KEY_SEG + p_i // SUBLANES
        perm = jnp.where(t_i == src, 1.0, 0.0).astype(BF16)
        for g in range(rows // KEY_BLOCK):
            blk = slice(g * KEY_BLOCK, (g + 1) * KEY_BLOCK)
            xp_ref[blk, :] = jnp.dot(perm, xb_ref[blk, :],
                                     preferred_element_type=F32).astype(BF16)
        k_ref[...] = jnp.dot(xp_ref[...], wk_ref[...],
                             preferred_element_type=F32).astype(k_ref.dtype)

    def values():
        vt_ref[...] = lax.dot_general(wvt_ref[...], xp_ref[...], nt,
                                      preferred_element_type=F32).astype(vt_ref.dtype)

    return pool_input, queries, keys, values


def _project(h2d, w_u, w_qt, w_k, w_vt):
    T, D = h2d.shape
    width = w_u.shape[1]
    row_major = lambda: pl.BlockSpec((ROW_TILE, width), lambda i: (i, 0))
    feat_major = lambda: pl.BlockSpec((width, ROW_TILE), lambda i: (0, i))
    full = lambda a: pl.BlockSpec(a.shape, lambda i: (0, 0))
    return pl.pallas_call(
        _proj_kernel,
        grid=(T // ROW_TILE,),
        in_specs=[pl.BlockSpec((ROW_TILE, D), lambda i: (i, 0)),
                  full(w_u), full(w_qt), full(w_k), full(w_vt)],
        out_specs=[row_major(), feat_major(), row_major(), feat_major()],
        out_shape=[jax.ShapeDtypeStruct((T, width), F32),
                   jax.ShapeDtypeStruct((width, T), BF16),
                   jax.ShapeDtypeStruct((T, width), BF16),
                   jax.ShapeDtypeStruct((width, T), BF16)],
        scratch_shapes=[pltpu.VMEM((ROW_TILE, D), BF16), pltpu.VMEM((ROW_TILE, D), BF16)],
        compiler_params=_params(("parallel",)),
        name="proj",
    )(h2d, w_u, w_qt, w_k, w_vt)


def _pool_kernel(u_ref, w_ref, s_ref, o_ref):
    S = u_ref.shape[0]
    row = lax.broadcasted_iota(jnp.int32, (S, LANES), 0)
    for g, win in enumerate(POOL_WINDOWS):
        u = u_ref[:, g * LANES:(g + 1) * LANES]
        acc = u
        k = 1
        while k < win:
            shifted = jnp.where(row >= k, pltpu.roll(acc, k, 0), 0.0)
            acc = acc + shifted
            k *= 2
        count = jnp.minimum(row + 1, win).astype(F32)
        pooled = acc / count - u
        mixed = jnp.dot(pooled.astype(BF16), w_ref[g], preferred_element_type=F32)
        o_ref[:, g * LANES:(g + 1) * LANES] = (
            mixed * s_ref[:, g * LANES:(g + 1) * LANES]).astype(o_ref.dtype)


def _pool(u, w_pool_b, pool_scale, B, S):
    T, W = u.shape
    return pl.pallas_call(
        _pool_kernel,
        grid=(B,),
        in_specs=[pl.BlockSpec((S, W), lambda b: (b, 0)),
                  pl.BlockSpec(w_pool_b.shape, lambda b: (0, 0, 0)),
                  pl.BlockSpec((1, W), lambda b: (0, 0))],
        out_specs=pl.BlockSpec((S, W), lambda b: (b, 0)),
        out_shape=jax.ShapeDtypeStruct((T, W), BF16),
        compiler_params=_params(("parallel",)),
        name="pool",
    )(u, w_pool_b, pool_scale)


def _attn_kernel(qt_ref, k_ref, vt_ref, o_ref, acc_ref, car_ref, z0_ref, z1_ref, a0_ref, a1_ref):
    i = pl.program_id(2)
    qb = QUERY_BLOCK
    kb = KEY_BLOCK
    width = 2 * qb
    pairs = range(PAIRS_PER_STEP)
    pair_rows = [slice(p * LANES, (p + 1) * LANES) for p in pairs]
    feat = lax.broadcasted_iota(jnp.int32, (LANES, qb), 0)
    rhs = []
    for p in pairs:
        q2 = qt_ref[pair_rows[p], :]
        zero = jnp.zeros_like(q2)
        rhs.append(jnp.concatenate([jnp.where(feat < SB_HEAD_DIM, q2, zero),
                                    jnp.where(feat >= SB_HEAD_DIM, q2, zero)], axis=1))
    sub = lax.broadcasted_iota(jnp.int32, (SUBLANES, width), 0)
    lane = lax.broadcasted_iota(jnp.int32, (SUBLANES, width), 1)
    q_pos = i * qb + jnp.bitwise_and(lane, qb - 1)

    acc_ref[...] = jnp.zeros_like(acc_ref)
    car_ref[...] = jnp.ones_like(car_ref)

    n_diag = qb // kb
    last = (i + 1) * n_diag - 1

    def block_start(n):
        return pl.multiple_of(jnp.clip(last - n, 0, last) * kb, kb)

    def scores(n, z_ref, ps=pairs):
        for p in ps:
            z_ref[p] = jnp.dot(k_ref[pl.ds(block_start(n), kb), pair_rows[p]], rhs[p],
                               preferred_element_type=F32)

    def values(n, a_ref, ps=pairs):
        for p in ps:
            acc_ref[p] += jnp.dot(vt_ref[pair_rows[p], pl.ds(block_start(n), kb)], a_ref[p],
                                  preferred_element_type=F32)

    def weights(n, z_ref, a_ref, masked, ps=pairs):
        for p in ps:
            weights_one(n, z_ref.at[p], a_ref.at[p], car_ref.at[p], masked)

    def weights_one(n, z_ref, a_ref, car_ref, masked):
        start = block_start(n)
        run = jnp.ones((SUBLANES, width), F32)
        for v in reversed(range(KEY_SEG)):
            slab = slice(v * SUBLANES, (v + 1) * SUBLANES)
            beta = 1.0 / (1.0 + jnp.exp(z_ref[slab, :]))
            rest = 1.0 - beta
            if masked:
                m = (start + sub * KEY_SEG + v) < q_pos
                beta = jnp.where(m, beta, 0.0)
                rest = jnp.where(m, rest, 1.0)
            z_ref[slab, :] = beta * run
            run = run * rest
        incl = run
        for step in (1, 2, 4):
            shifted = pltpu.roll(incl, SUBLANES - step, 0)
            incl = incl * jnp.where(sub < SUBLANES - step, shifted, 1.0)
        later = jnp.where(sub < SUBLANES - 1, pltpu.roll(incl, SUBLANES - 1, 0), 1.0)
        base = later * car_ref[...]
        a_ref[...] = jnp.concatenate(
            [z_ref[v * SUBLANES:(v + 1) * SUBLANES, :] * base for v in range(KEY_SEG)],
            axis=0).astype(BF16)
        car_ref[...] *= jnp.broadcast_to(incl[0:1, :], (SUBLANES, width))

    assert n_diag == 2
    scores(0, z0_ref)
    scores(1, z1_ref)
    weights(0, z0_ref, a0_ref, True)
    scores(2, z0_ref)
    weights(1, z1_ref, a1_ref, True)
    values(0, a0_ref)

    for g in range(0, PAIRS_PER_STEP, LOOP_PAIRS):
        ps = tuple(range(g, g + LOOP_PAIRS))

        def two_blocks(t, carry, ps=ps):
            n = 2 + 2 * t
            values(n - 1, a1_ref, ps)
            weights(n, z0_ref, a0_ref, False, ps)
            scores(n + 1, z1_ref, ps)
            values(n, a0_ref, ps)
            weights(n + 1, z1_ref, a1_ref, False, ps)
            scores(n + 2, z0_ref, ps)
            return carry

        lax.fori_loop(0, i, two_blocks, 0)
    values(last, a1_ref)

    for p in pairs:
        acc = acc_ref[p]
        both = jnp.where(feat < SB_HEAD_DIM, acc[:, :qb], acc[:, qb:])
        o_ref[:, pair_rows[p]] = both.T.astype(o_ref.dtype)


def _attention(q_t, k_perm, v_t, B, S):
    W, T = q_t.shape
    qb = QUERY_BLOCK
    nq = S // qb
    step_w = PAIRS_PER_STEP * LANES
    groups = W // step_w
    per_pair = lambda shape, dtype: pltpu.VMEM((PAIRS_PER_STEP,) + shape, dtype)
    return pl.pallas_call(
        _attn_kernel,
        grid=(B, groups, nq),
        in_specs=[pl.BlockSpec((step_w, qb), lambda b, g, i: (g, b * nq + i)),
                  pl.BlockSpec((S, step_w), lambda b, g, i: (b, g)),
                  pl.BlockSpec((step_w, S), lambda b, g, i: (g, b))],
        out_specs=pl.BlockSpec((qb, step_w), lambda b, g, i: (b * nq + i, g)),
        out_shape=jax.ShapeDtypeStruct((T, W), BF16),
        scratch_shapes=[per_pair((LANES, 2 * qb), F32), per_pair((SUBLANES, 2 * qb), F32),
                        per_pair((KEY_BLOCK, 2 * qb), F32), per_pair((KEY_BLOCK, 2 * qb), F32),
                        per_pair((KEY_BLOCK, 2 * qb), BF16), per_pair((KEY_BLOCK, 2 * qb), BF16)],
        compiler_params=_params(("parallel", "parallel", "parallel")),
        name="attn",
    )(q_t, k_perm, v_t)


def _layer_norm(y, g, b):
    mu = jnp.mean(y, axis=-1, keepdims=True)
    yc = y - mu
    var = jnp.mean(yc * yc, axis=-1, keepdims=True)
    return yc * lax.rsqrt(var + LN_EPS) * g + b


def _mix_kernel(alpha, yp_ref, ys_ref, h_ref, wo_ref, g_ref, b_ref, wr_ref, br_ref,
                h1_ref, route_ref, cnt_ref, run_ref):
    @pl.when(pl.program_id(0) == 0)
    def _():
        run_ref[...] = jnp.zeros_like(run_ref)

    half = yp_ref.shape[1]
    mix = (jnp.dot(yp_ref[...], wo_ref[:half, :], preferred_element_type=F32) +
           jnp.dot(ys_ref[...], wo_ref[half:, :], preferred_element_type=F32))
    h1 = _layer_norm(alpha * h_ref[...] + mix, g_ref[...], b_ref[...])
    h1_ref[...] = h1

    hh = h1.astype(BF16)
    hl = (h1 - hh.astype(F32)).astype(BF16)
    prod_h = jnp.dot(hh, wr_ref[...], preferred_element_type=F32)
    prod_l = jnp.dot(hl, wr_ref[:, :LANES], preferred_element_type=F32)
    logits = prod_h[:, :LANES] + prod_l + prod_h[:, LANES:] + br_ref[...]
    col = lax.broadcasted_iota(jnp.int32, logits.shape, 1)
    gl = jnp.where(col < N_EXPERT_GROUPS, logits, NEG_BIG)
    gmax = jnp.max(gl, axis=-1, keepdims=True)
    gidx = jnp.min(jnp.where(gl == gmax, col, LANES), axis=-1, keepdims=True)
    g_p = 1.0 / jnp.sum(jnp.exp(gl - gmax), axis=-1, keepdims=True)
    lo_col = N_EXPERT_GROUPS + EXPERTS_PER_GROUP * gidx
    el = jnp.where(col >= lo_col, jnp.where(col < lo_col + EXPERTS_PER_GROUP, logits, NEG_BIG),
                   NEG_BIG)
    m1 = jnp.max(el, axis=-1, keepdims=True)
    i1 = jnp.min(jnp.where(el == m1, col, LANES), axis=-1, keepdims=True)
    el2 = jnp.where(col == i1, NEG_BIG, el)
    m2 = jnp.max(el2, axis=-1, keepdims=True)
    i2 = jnp.min(jnp.where(el2 == m2, col, LANES), axis=-1, keepdims=True)
    ratio = jnp.exp(m2 - m1)
    gate1 = g_p / (1.0 + ratio)
    gate2 = g_p * ratio / (1.0 + ratio)
    e1 = (i1 - N_EXPERT_GROUPS).astype(F32)
    e2 = (i2 - N_EXPERT_GROUPS).astype(F32)

    rows = logits.shape[0]
    hit1 = col == i1
    hit2 = col == i2
    onehot = jnp.where(hit1, 1.0, jnp.where(hit2, 1.0, 0.0))
    r_i = lax.broadcasted_iota(jnp.int32, (rows, rows), 0)
    c_i = lax.broadcasted_iota(jnp.int32, (rows, rows), 1)
    earlier = jnp.where(c_i < r_i, 1.0, 0.0).astype(BF16)
    before = run_ref[...] + jnp.dot(earlier, onehot.astype(BF16), preferred_element_type=F32)
    rank1 = jnp.sum(jnp.where(hit1, before, 0.0), axis=-1, keepdims=True)
    rank2 = jnp.sum(jnp.where(hit2, before, 0.0), axis=-1, keepdims=True)
    run_ref[...] += jnp.sum(onehot, axis=0, keepdims=True)
    cnt_ref[...] = run_ref[...]

    route_ref[...] = jnp.where(col == 0, e1, jnp.where(col == 1, e2,
                               jnp.where(col == 2, gate1, jnp.where(col == 3, gate2,
                               jnp.where(col == 4, rank1, jnp.where(col == 5, rank2, 0.0))))))


def _mix_norm_route(alpha, y_pool, y_sb, h2d, w_out_b, g, b, w_r2, b_r):
    T, D = h2d.shape
    half = y_pool.shape[1]
    full = lambda a: pl.BlockSpec(a.shape, lambda i: (0,) * a.ndim)
    return pl.pallas_call(
        functools.partial(_mix_kernel, alpha),
        grid=(T // ROW_TILE,),
        in_specs=[pl.BlockSpec((ROW_TILE, half), lambda i: (i, 0)),
                  pl.BlockSpec((ROW_TILE, half), lambda i: (i, 0)),
                  pl.BlockSpec((ROW_TILE, D), lambda i: (i, 0)),
                  full(w_out_b), full(g), full(b), full(w_r2), full(b_r)],
        out_specs=[pl.BlockSpec((ROW_TILE, D), lambda i: (i, 0)),
                   pl.BlockSpec((ROW_TILE, LANES), lambda i: (i, 0)),
                   pl.BlockSpec((1, LANES), lambda i: (0, 0))],
        out_shape=[jax.ShapeDtypeStruct((T, D), F32), jax.ShapeDtypeStruct((T, LANES), F32),
                   jax.ShapeDtypeStruct((1, LANES), F32)],
        scratch_shapes=[pltpu.VMEM((1, LANES), F32)],
        compiler_params=_params(("arbitrary",)),
        name="mix_norm_route",
    )(y_pool, y_sb, h2d, w_out_b, g, b, w_r2, b_r)


def _row_gather_copy(src_hbm, row, dst_ref, dst_row, sem):
    return pltpu.make_async_copy(src_hbm.at[pl.ds(row, 1)], dst_ref.at[pl.ds(dst_row, 1)], sem)


def _row_scatter_copy(src_ref, src_row, dst_hbm, row, sem):
    return pltpu.make_async_copy(src_ref.at[pl.ds(src_row, 1)], dst_hbm.at[pl.ds(row, 1)], sem)


def _dispatch_kernel(pos_ref, tail_ref, nvb_ref, h_ref, xs_hbm, sbuf, zbuf, sem, zsem):
    i = pl.program_id(0)
    nb = pl.num_programs(0)
    rows = DISPATCH_TILE
    blk = DISPATCH_ROWS
    n_blocks = xs_hbm.shape[0] // blk

    def zero_copy(start):
        return pltpu.make_async_copy(zbuf, xs_hbm.at[pl.ds(pl.multiple_of(start, blk), blk)], zsem)

    @pl.when(i == 0)
    def _():
        zbuf[...] = jnp.zeros_like(zbuf)
        for e in range(N_EXPERTS):
            @pl.when(tail_ref[e] >= 0)
            def _():
                zero_copy(tail_ref[e]).start()

        def start_unused(b, carry):
            zero_copy(b * blk).start()
            return carry

        def wait_unused(b, carry):
            zero_copy(b * blk).wait()
            return carry

        lax.fori_loop(nvb_ref[0], n_blocks, start_unused, 0)
        for e in range(N_EXPERTS):
            @pl.when(tail_ref[e] >= 0)
            def _():
                zero_copy(tail_ref[e]).wait()
        lax.fori_loop(nvb_ref[0], n_blocks, wait_unused, 0)

    slot = i % 2

    def wait_slot(s):
        for _ in range(TOP_K):
            pltpu.make_async_copy(sbuf.at[s], xs_hbm.at[pl.ds(0, rows)], sem.at[s]).wait()

    @pl.when(i >= 2)
    def _():
        wait_slot(slot)

    sbuf[slot] = h_ref[...]

    def body(r, carry):
        for k in range(TOP_K):
            _row_scatter_copy(sbuf.at[slot], r, xs_hbm, pos_ref[(i * rows + r) * TOP_K + k],
                              sem.at[slot]).start(priority=k)
        return carry

    lax.fori_loop(0, rows, body, 0, unroll=8)

    @pl.when(i == nb - 1)
    def _():
        @pl.when(nb >= 2)
        def _():
            wait_slot(1 - slot)
        wait_slot(slot)


def _dispatch(pos, tail, n_valid_blocks, h1, n_blocks):
    T, D = h1.shape
    rows = DISPATCH_TILE
    grid_spec = pltpu.PrefetchScalarGridSpec(
        num_scalar_prefetch=3,
        grid=(T // rows,),
        in_specs=[pl.BlockSpec((rows, D), lambda i, pos, tail, nvb: (i, 0))],
        out_specs=pl.BlockSpec(memory_space=pl.ANY),
        scratch_shapes=[pltpu.VMEM((2, rows, D), F32), pltpu.VMEM((DISPATCH_ROWS, D), F32),
                        pltpu.SemaphoreType.DMA((2,)), pltpu.SemaphoreType.DMA],
    )
    return pl.pallas_call(
        _dispatch_kernel,
        grid_spec=grid_spec,
        out_shape=jax.ShapeDtypeStruct((n_blocks * DISPATCH_ROWS, D), F32),
        compiler_params=_params(("arbitrary",)),
        name="dispatch",
    )(pos, tail, n_valid_blocks, h1)


def _expert_kernel(be_ref, nvb_ref, xs_ref, wg_ref, wu_ref, wd_ref, ys_ref, wgb, wub, wdb):
    i = pl.program_id(0)
    e = be_ref[i]
    prev = be_ref[jnp.maximum(i - 1, 0)]

    @pl.when(jnp.logical_or(i == 0, e != prev))
    def _():
        wgb[...] = wg_ref[...].astype(BF16)
        wub[...] = wu_ref[...].astype(BF16)
        wdb[...] = wd_ref[...].astype(BF16)

    @pl.when(i < nvb_ref[0])
    def _():
        x = xs_ref[...].astype(BF16)
        gate = jnp.dot(x, wgb[...], preferred_element_type=F32)
        up = jnp.dot(x, wub[...], preferred_element_type=F32)
        hid = (gate * jax.nn.sigmoid(gate) * up).astype(BF16)
        ys_ref[...] = jnp.dot(hid, wdb[...], preferred_element_type=F32)

    @pl.when(i >= nvb_ref[0])
    def _():
        ys_ref[...] = jnp.zeros_like(ys_ref)


def _experts(layer, block_e, n_valid_blocks, xs, w_gate, w_up, w_down):
    P, D = xs.shape
    DE = w_gate.shape[-1]
    rows = DISPATCH_ROWS
    n_blocks = P // rows
    x_map = lambda i, be, nvb: (jnp.minimum(i, jnp.maximum(nvb[0] - 1, 0)), 0)
    grid_spec = pltpu.PrefetchScalarGridSpec(
        num_scalar_prefetch=2,
        grid=(n_blocks,),
        in_specs=[pl.BlockSpec((rows, D), x_map),
                  pl.BlockSpec((None, None, D, DE), lambda i, be, nvb: (layer, be[i], 0, 0)),
                  pl.BlockSpec((None, None, D, DE), lambda i, be, nvb: (layer, be[i], 0, 0)),
                  pl.BlockSpec((None, None, DE, D), lambda i, be, nvb: (layer, be[i], 0, 0))],
        out_specs=pl.BlockSpec((rows, D), lambda i, be, nvb: (i, 0)),
        scratch_shapes=[pltpu.VMEM((D, DE), BF16), pltpu.VMEM((D, DE), BF16),
                        pltpu.VMEM((DE, D), BF16)],
    )
    return pl.pallas_call(
        _expert_kernel,
        grid_spec=grid_spec,
        out_shape=jax.ShapeDtypeStruct((P, D), F32),
        compiler_params=_params(("arbitrary",)),
        name="experts",
    )(block_e, n_valid_blocks, xs, w_gate, w_up, w_down)


def _combine_kernel(alpha, with_proj, pos_ref, ys_hbm, h_ref, route_ref, g_ref, b_ref, *refs):
    if with_proj:
        w_refs, o_ref, out_refs = refs[0:4], refs[4], refs[5:9]
        buf_a, buf_b, sem, fence_sem, xb_ref, xp_ref = refs[9:]
    else:
        o_ref, buf_a, buf_b, sem, fence_sem = refs
    i = pl.program_id(0)
    nb = pl.num_programs(0)
    tile = o_ref.shape[0]
    half = tile // 2
    bufs = (buf_a, buf_b)

    def issue(tile_idx, part, lo, hi):
        base = tile_idx * tile + part * half
        for r in range(lo, hi):
            for k in range(TOP_K):
                _row_gather_copy(ys_hbm, pos_ref[(base + r) * TOP_K + k], bufs[part].at[k], r,
                                 sem.at[part]).start(priority=k)

    def wait(part):
        for k in range(TOP_K):
            pltpu.make_async_copy(ys_hbm.at[pl.ds(0, half)], bufs[part].at[k],
                                  sem.at[part]).wait()

    def finish(part):
        rows = slice(part * half, (part + 1) * half)
        route = route_ref[rows, :]
        moe = route[:, 2:3] * bufs[part][0] + route[:, 3:4] * bufs[part][1]
        o_ref[rows, :] = _layer_norm(alpha * h_ref[rows, :] + moe, g_ref[...], b_ref[...])

    def fence():
        pl.semaphore_signal(fence_sem, 1)
        pl.semaphore_wait(fence_sem, 1)

    @pl.when(i == 0)
    def _():
        issue(0, 0, 0, half)
        issue(0, 1, 0, half)

    nxt = jnp.minimum(i + 1, nb - 1)
    wait(0)
    wait(1)
    finish(0)
    fence()
    issue(nxt, 0, 0, half)
    finish(1)
    fence()
    issue(nxt, 1, 0, half)
    if with_proj:
        stages = _projection_stages(o_ref, w_refs, out_refs, xb_ref, xp_ref)
        stages[0]()
        stages[1]()
        fence()
        for stage in stages[2:]:
            stage()

    @pl.when(i == nb - 1)
    def _():
        wait(0)
        wait(1)


def _combine_norm(alpha, pos, ys, h1, route, g, b, proj_weights=None):
    T, D = h1.shape
    rows = ROW_TILE
    with_proj = proj_weights is not None
    const = lambda a: pl.BlockSpec(a.shape, lambda i, pos: (0,) * a.ndim)
    in_specs = [pl.BlockSpec(memory_space=pl.ANY),
                pl.BlockSpec((rows, D), lambda i, pos: (i, 0)),
                pl.BlockSpec((rows, LANES), lambda i, pos: (i, 0)),
                const(g), const(b)]
    out_specs = [pl.BlockSpec((rows, D), lambda i, pos: (i, 0))]
    out_shape = [jax.ShapeDtypeStruct((T, D), F32)]
    args = [pos, ys, h1, route, g, b]
    if with_proj:
        width = proj_weights[0].shape[1]
        in_specs += [const(w) for w in proj_weights]
        args += list(proj_weights)
        row_major = lambda: pl.BlockSpec((rows, width), lambda i, pos: (i, 0))
        feat_major = lambda: pl.BlockSpec((width, rows), lambda i, pos: (0, i))
        out_specs += [row_major(), feat_major(), row_major(), feat_major()]
        out_shape += [jax.ShapeDtypeStruct((T, width), F32), jax.ShapeDtypeStruct((width, T), BF16),
                      jax.ShapeDtypeStruct((T, width), BF16), jax.ShapeDtypeStruct((width, T), BF16)]
    grid_spec = pltpu.PrefetchScalarGridSpec(
        num_scalar_prefetch=1,
        grid=(T // rows,),
        in_specs=in_specs,
        out_specs=out_specs,
        scratch_shapes=[pltpu.VMEM((TOP_K, rows // 2, D), F32), pltpu.VMEM((TOP_K, rows // 2, D), F32),
                        pltpu.SemaphoreType.DMA((2,)), pltpu.SemaphoreType.REGULAR] +
                       ([pltpu.VMEM((rows, D), BF16), pltpu.VMEM((rows, D), BF16)] if with_proj else []),
    )
    out = pl.pallas_call(
        functools.partial(_combine_kernel, alpha, with_proj),
        grid_spec=grid_spec,
        out_shape=out_shape,
        compiler_params=_params(("arbitrary",)),
        name="combine_norm_proj" if with_proj else "combine_norm",
    )(*args)
    return out if with_proj else out[0]


def _dispatch_plan(route, counts_row, n_blocks):
    rows = DISPATCH_ROWS
    experts = jnp.arange(N_EXPERTS, dtype=jnp.int32)
    counts = counts_row[0, N_EXPERT_GROUPS:N_EXPERT_GROUPS + N_EXPERTS].astype(jnp.int32)
    padded = ((counts + rows - 1) // rows) * rows
    pend = jnp.cumsum(padded)
    pstart = pend - padded
    block_start = jnp.arange(n_blocks, dtype=jnp.int32) * rows
    block_e = jnp.minimum(jnp.sum((block_start[:, None] >= pend[None, :]).astype(jnp.int32), axis=1),
                          N_EXPERTS - 1).astype(jnp.int32)
    n_valid_blocks = (pend[-1:] // rows).astype(jnp.int32)
    tail = jnp.where(counts > 0, pend - rows, -1).astype(jnp.int32)
    e = route[:, :TOP_K].astype(jnp.int32)
    rank = route[:, 4:4 + TOP_K].astype(jnp.int32)
    base = jnp.sum(jnp.where(e[:, :, None] == experts[None, None, :], pstart[None, None, :], 0),
                   axis=-1)
    pos = (base + rank).reshape(-1).astype(jnp.int32)
    return block_e, n_valid_blocks, tail, pos


def kernel(x, w_in, w_pool, pool_scale, w_out, ln1_g, ln1_b, w_router_group, b_router_group,
           w_router_expert, b_router_expert, w_gate, w_up, w_down, ln2_g, ln2_b):
    B, S, D = x.shape
    depth = w_in.shape[0]
    T = B * S
    alpha = (2.0 * depth) ** 0.25
    pool_width = w_pool.shape[1] * w_pool.shape[2]
    sb_width = (w_in.shape[2] - pool_width) // 3
    n_blocks = -(-T * TOP_K // DISPATCH_ROWS) + N_EXPERTS

    col_scale = jnp.concatenate([jnp.ones((pool_width,), F32),
                                 jnp.full((sb_width,), -(SB_HEAD_DIM ** -0.5), F32),
                                 jnp.ones((2 * sb_width,), F32)])
    w_in_s = w_in * col_scale
    w_u = w_in_s[:, :, :pool_width].astype(BF16)
    w_qt = jnp.swapaxes(w_in_s[:, :, pool_width:pool_width + sb_width], 1, 2).astype(BF16)
    w_k = w_in_s[:, :, pool_width + sb_width:pool_width + 2 * sb_width].astype(BF16)
    w_vt = jnp.swapaxes(w_in_s[:, :, pool_width + 2 * sb_width:], 1, 2).astype(BF16)
    w_pool_b = w_pool.astype(BF16)
    w_out_b = w_out.astype(BF16)
    w_r = jnp.concatenate([w_router_group, w_router_expert,
                           jnp.zeros((depth, D, LANES - N_EXPERT_GROUPS - N_EXPERTS), F32)], axis=-1)
    wr_hi = w_r.astype(BF16)
    wr_lo = (w_r - wr_hi.astype(F32)).astype(BF16)
    w_r2 = jnp.concatenate([wr_hi, wr_lo], axis=-1)
    b_r = jnp.concatenate([b_router_group.astype(F32),
                           b_router_expert.astype(F32).reshape(depth, N_EXPERTS),
                           jnp.zeros((depth, LANES - N_EXPERT_GROUPS - N_EXPERTS), F32)], axis=-1)

    h = x.reshape(T, D)
    u, q_t, k_perm, v_t = _project(h, w_u[0], w_qt[0], w_k[0], w_vt[0])
    for l in range(depth):
        y_pool = _pool(u, w_pool_b[l], pool_scale[l][None, :], B, S)
        y_sb = _attention(q_t, k_perm, v_t, B, S)
        h1, route, counts = _mix_norm_route(alpha, y_pool, y_sb, h, w_out_b[l], ln1_g[l][None, :],
                                            ln1_b[l][None, :], w_r2[l], b_r[l][None, :])
        block_e, n_valid_blocks, tail, pos = _dispatch_plan(route, counts, n_blocks)
        xs = _dispatch(pos, tail, n_valid_blocks, h1, n_blocks)
        ys = _experts(l, block_e, n_valid_blocks, xs, w_gate, w_up, w_down)
        nxt = None if l + 1 == depth else (w_u[l + 1], w_qt[l + 1], w_k[l + 1], w_vt[l + 1])
        out = _combine_norm(alpha, pos, ys, h1, route, ln2_g[l][None, :], ln2_b[l][None, :], nxt)
        if nxt is None:
            h = out
        else:
            h, u, q_t, k_perm, v_t = out
    return h.reshape(B, S, D)
```

```python
import functools

import jax
import jax.numpy as jnp
from jax import lax
from jax.experimental import pallas as pl
from jax.experimental.pallas import tpu as pltpu

F32 = jnp.float32
BF16 = jnp.bfloat16

POOL_WINDOWS = (2, 4, 8, 16)
SB_HEAD_DIM = 64
N_EXPERT_GROUPS = 4
EXPERTS_PER_GROUP = 8
N_EXPERTS = N_EXPERT_GROUPS * EXPERTS_PER_GROUP
TOP_K = 2
LN_EPS = 1e-5

LANES = 128
SUBLANES = 8
ROW_TILE = 512
QUERY_BLOCK = 256
KEY_BLOCK = 128
KEY_SEG = KEY_BLOCK // SUBLANES
PAIRS_PER_STEP = 4
LOOP_PAIRS = 2
DISPATCH_ROWS = 256
NEG_BIG = -1e30
VMEM_LIMIT = 48 * 1024 * 1024


def _params(sem):
    return pltpu.CompilerParams(dimension_semantics=sem, vmem_limit_bytes=VMEM_LIMIT)


def _proj_kernel(x_ref, wu_ref, wqt_ref, wk_ref, wvt_ref, u_ref, qt_ref, k_ref, vt_ref,
                 xb_ref, xp_ref):
    for stage in _projection_stages(x_ref, (wu_ref, wqt_ref, wk_ref, wvt_ref),
                                    (u_ref, qt_ref, k_ref, vt_ref), xb_ref, xp_ref):
        stage()


def _projection_stages(x_ref, w_refs, out_refs, xb_ref, xp_ref):
    wu_ref, wqt_ref, wk_ref, wvt_ref = w_refs
    u_ref, qt_ref, k_ref, vt_ref = out_refs
    rows = x_ref.shape[0]
    nt = (((1,), (1,)), ((), ()))

    def pool_input():
        xb_ref[...] = x_ref[...].astype(BF16)
        u_ref[...] = jnp.dot(xb_ref[...], wu_ref[...], preferred_element_type=F32)

    def queries():
        qt_ref[...] = lax.dot_general(wqt_ref[...], xb_ref[...], nt,
                                      preferred_element_type=F32).astype(qt_ref.dtype)

    def keys():
        p_i = lax.broadcasted_iota(jnp.int32, (KEY_BLOCK, KEY_BLOCK), 0)
        t_i = lax.broadcasted_iota(jnp.int32, (KEY_BLOCK, KEY_BLOCK), 1)
        src = (p_i % SUBLANES) * KEY_SEG + p_i // SUBLANES
        perm = jnp.where(t_i == src, 1.0, 0.0).astype(BF16)
        for g in range(rows // KEY_BLOCK):
            blk = slice(g * KEY_BLOCK, (g + 1) * KEY_BLOCK)
            xp_ref[blk, :] = jnp.dot(perm, xb_ref[blk, :],
                                     preferred_element_type=F32).astype(BF16)
        k_ref[...] = jnp.dot(xp_ref[...], wk_ref[...],
                             preferred_element_type=F32).astype(k_ref.dtype)

    def values():
        vt_ref[...] = lax.dot_general(wvt_ref[...], xp_ref[...], nt,
                                      preferred_element_type=F32).astype(vt_ref.dtype)

    return pool_input, queries, keys, values


def _project(h2d, w_u, w_qt, w_k, w_vt):
    T, D = h2d.shape
    width = w_u.shape[1]
    row_major = lambda: pl.BlockSpec((ROW_TILE, width), lambda i: (i, 0))
    feat_major = lambda: pl.BlockSpec((width, ROW_TILE), lambda i: (0, i))
    full = lambda a: pl.BlockSpec(a.shape, lambda i: (0, 0))
    return pl.pallas_call(
        _proj_kernel,
        grid=(T // ROW_TILE,),
        in_specs=[pl.BlockSpec((ROW_TILE, D), lambda i: (i, 0)),
                  full(w_u), full(w_qt), full(w_k), full(w_vt)],
        out_specs=[row_major(), feat_major(), row_major(), feat_major()],
        out_shape=[jax.ShapeDtypeStruct((T, width), F32),
                   jax.ShapeDtypeStruct((width, T), BF16),
                   jax.ShapeDtypeStruct((T, width), BF16),
                   jax.ShapeDtypeStruct((width, T), BF16)],
        scratch_shapes=[pltpu.VMEM((ROW_TILE, D), BF16), pltpu.VMEM((ROW_TILE, D), BF16)],
        compiler_params=_params(("parallel",)),
        name="proj",
    )(h2d, w_u, w_qt, w_k, w_vt)


def _pool_kernel(u_ref, w_ref, s_ref, o_ref):
    S = u_ref.shape[0]
    row = lax.broadcasted_iota(jnp.int32, (S, LANES), 0)
    for g, win in enumerate(POOL_WINDOWS):
        u = u_ref[:, g * LANES:(g + 1) * LANES]
        acc = u
        k = 1
        while k < win:
            shifted = jnp.where(row >= k, pltpu.roll(acc, k, 0), 0.0)
            acc = acc + shifted
            k *= 2
        count = jnp.minimum(row + 1, win).astype(F32)
        pooled = acc / count - u
        mixed = jnp.dot(pooled.astype(BF16), w_ref[g], preferred_element_type=F32)
        o_ref[:, g * LANES:(g + 1) * LANES] = (
            mixed * s_ref[:, g * LANES:(g + 1) * LANES]).astype(o_ref.dtype)


def _pool(u, w_pool_b, pool_scale, B, S):
    T, W = u.shape
    return pl.pallas_call(
        _pool_kernel,
        grid=(B,),
        in_specs=[pl.BlockSpec((S, W), lambda b: (b, 0)),
                  pl.BlockSpec(w_pool_b.shape, lambda b: (0, 0, 0)),
                  pl.BlockSpec((1, W), lambda b: (0, 0))],
        out_specs=pl.BlockSpec((S, W), lambda b: (b, 0)),
        out_shape=jax.ShapeDtypeStruct((T, W), BF16),
        compiler_params=_params(("parallel",)),
        name="pool",
    )(u, w_pool_b, pool_scale)


def _attn_kernel(qt_ref, k_ref, vt_ref, o_ref, acc_ref, car_ref, z0_ref, z1_ref, a0_ref, a1_ref):
    i = pl.program_id(2)
    qb = QUERY_BLOCK
    kb = KEY_BLOCK
    width = 2 * qb
    pairs = range(PAIRS_PER_STEP)
    pair_rows = [slice(p * LANES, (p + 1) * LANES) for p in pairs]
    feat = lax.broadcasted_iota(jnp.int32, (LANES, qb), 0)
    rhs = []
    for p in pairs:
        q2 = qt_ref[pair_rows[p], :]
        zero = jnp.zeros_like(q2)
        rhs.append(jnp.concatenate([jnp.where(feat < SB_HEAD_DIM, q2, zero),
                                    jnp.where(feat >= SB_HEAD_DIM, q2, zero)], axis=1))
    sub = lax.broadcasted_iota(jnp.int32, (SUBLANES, width), 0)
    lane = lax.broadcasted_iota(jnp.int32, (SUBLANES, width), 1)
    q_pos = i * qb + jnp.bitwise_and(lane, qb - 1)

    acc_ref[...] = jnp.zeros_like(acc_ref)
    car_ref[...] = jnp.ones_like(car_ref)

    n_diag = qb // kb
    last = (i + 1) * n_diag - 1

    def block_start(n):
        return pl.multiple_of(jnp.clip(last - n, 0, last) * kb, kb)

    def scores(n, z_ref, ps=pairs):
        for p in ps:
            z_ref[p] = jnp.dot(k_ref[pl.ds(block_start(n), kb), pair_rows[p]], rhs[p],
                               preferred_element_type=F32)

    def values(n, a_ref, ps=pairs):
        for p in ps:
            acc_ref[p] += jnp.dot(vt_ref[pair_rows[p], pl.ds(block_start(n), kb)], a_ref[p],
                                  preferred_element_type=F32)

    def weights(n, z_ref, a_ref, masked, ps=pairs):
        for p in ps:
            weights_one(n, z_ref.at[p], a_ref.at[p], car_ref.at[p], masked)

    def weights_one(n, z_ref, a_ref, car_ref, masked):
        start = block_start(n)
        run = jnp.ones((SUBLANES, width), F32)
        for v in reversed(range(KEY_SEG)):
            slab = slice(v * SUBLANES, (v + 1) * SUBLANES)
            beta = 1.0 / (1.0 + jnp.exp(z_ref[slab, :]))
            rest = 1.0 - beta
            if masked:
                m = (start + sub * KEY_SEG + v) < q_pos
                beta = jnp.where(m, beta, 0.0)
                rest = jnp.where(m, rest, 1.0)
            z_ref[slab, :] = beta * run
            run = run * rest
        incl = run
        for step in (1, 2, 4):
            shifted = pltpu.roll(incl, SUBLANES - step, 0)
            incl = incl * jnp.where(sub < SUBLANES - step, shifted, 1.0)
        later = jnp.where(sub < SUBLANES - 1, pltpu.roll(incl, SUBLANES - 1, 0), 1.0)
        base = later * car_ref[...]
        a_ref[...] = jnp.concatenate(
            [z_ref[v * SUBLANES:(v + 1) * SUBLANES, :] * base for v in range(KEY_SEG)],
            axis=0).astype(BF16)
        car_ref[...] *= jnp.broadcast_to(incl[0:1, :], (SUBLANES, width))

    assert n_diag == 2
    scores(0, z0_ref)
    scores(1, z1_ref)
    weights(0, z0_ref, a0_ref, True)
    scores(2, z0_ref)
    weights(1, z1_ref, a1_ref, True)
    values(0, a0_ref)

    for g in range(0, PAIRS_PER_STEP, LOOP_PAIRS):
        ps = tuple(range(g, g + LOOP_PAIRS))

        def two_blocks(t, carry, ps=ps):
            n = 2 + 2 * t
            values(n - 1, a1_ref, ps)
            weights(n, z0_ref, a0_ref, False, ps)
            scores(n + 1, z1_ref, ps)
            values(n, a0_ref, ps)
            weights(n + 1, z1_ref, a1_ref, False, ps)
            scores(n + 2, z0_ref, ps)
            return carry

        lax.fori_loop(0, i, two_blocks, 0)
    values(last, a1_ref)

    for p in pairs:
        acc = acc_ref[p]
        both = jnp.where(feat < SB_HEAD_DIM, acc[:, :qb], acc[:, qb:])
        o_ref[:, pair_rows[p]] = both.T.astype(o_ref.dtype)


def _attention(q_t, k_perm, v_t, B, S):
    W, T = q_t.shape
    qb = QUERY_BLOCK
    nq = S // qb
    step_w = PAIRS_PER_STEP * LANES
    groups = W // step_w
    per_pair = lambda shape, dtype: pltpu.VMEM((PAIRS_PER_STEP,) + shape, dtype)
    return pl.pallas_call(
        _attn_kernel,
        grid=(B, groups, nq),
        in_specs=[pl.BlockSpec((step_w, qb), lambda b, g, i: (g, b * nq + i)),
                  pl.BlockSpec((S, step_w), lambda b, g, i: (b, g)),
                  pl.BlockSpec((step_w, S), lambda b, g, i: (g, b))],
        out_specs=pl.BlockSpec((qb, step_w), lambda b, g, i: (b * nq + i, g)),
        out_shape=jax.ShapeDtypeStruct((T, W), BF16),
        scratch_shapes=[per_pair((LANES, 2 * qb), F32), per_pair((SUBLANES, 2 * qb), F32),
                        per_pair((KEY_BLOCK, 2 * qb), F32), per_pair((KEY_BLOCK, 2 * qb), F32),
                        per_pair((KEY_BLOCK, 2 * qb), BF16), per_pair((KEY_BLOCK, 2 * qb), BF16)],
        compiler_params=_params(("parallel", "parallel", "parallel")),
        name="attn",
    )(q_t, k_perm, v_t)


def _layer_norm(y, g, b):
    mu = jnp.mean(y, axis=-1, keepdims=True)
    yc = y - mu
    var = jnp.mean(yc * yc, axis=-1, keepdims=True)
    return yc * lax.rsqrt(var + LN_EPS) * g + b


def _mix_kernel(alpha, yp_ref, ys_ref, h_ref, wo_ref, g_ref, b_ref, wr_ref, br_ref,
                h1_ref, route_ref, cnt_ref, run_ref):
    @pl.when(pl.program_id(0) == 0)
    def _():
        run_ref[...] = jnp.zeros_like(run_ref)

    half = yp_ref.shape[1]
    mix = (jnp.dot(yp_ref[...], wo_ref[:half, :], preferred_element_type=F32) +
           jnp.dot(ys_ref[...], wo_ref[half:, :], preferred_element_type=F32))
    h1 = _layer_norm(alpha * h_ref[...] + mix, g_ref[...], b_ref[...])
    h1_ref[...] = h1

    hh = h1.astype(BF16)
    hl = (h1 - hh.astype(F32)).astype(BF16)
    prod_h = jnp.dot(hh, wr_ref[...], preferred_element_type=F32)
    prod_l = jnp.dot(hl, wr_ref[:, :LANES], preferred_element_type=F32)
    logits = prod_h[:, :LANES] + prod_l + prod_h[:, LANES:] + br_ref[...]
    col = lax.broadcasted_iota(jnp.int32, logits.shape, 1)
    gl = jnp.where(col < N_EXPERT_GROUPS, logits, NEG_BIG)
    gmax = jnp.max(gl, axis=-1, keepdims=True)
    gidx = jnp.min(jnp.where(gl == gmax, col, LANES), axis=-1, keepdims=True)
    g_p = 1.0 / jnp.sum(jnp.exp(gl - gmax), axis=-1, keepdims=True)
    lo_col = N_EXPERT_GROUPS + EXPERTS_PER_GROUP * gidx
    el = jnp.where(col >= lo_col, jnp.where(col < lo_col + EXPERTS_PER_GROUP, logits, NEG_BIG),
                   NEG_BIG)
    m1 = jnp.max(el, axis=-1, keepdims=True)
    i1 = jnp.min(jnp.where(el == m1, col, LANES), axis=-1, keepdims=True)
    el2 = jnp.where(col == i1, NEG_BIG, el)
    m2 = jnp.max(el2, axis=-1, keepdims=True)
    i2 = jnp.min(jnp.where(el2 == m2, col, LANES), axis=-1, keepdims=True)
    ratio = jnp.exp(m2 - m1)
    gate1 = g_p / (1.0 + ratio)
    gate2 = g_p * ratio / (1.0 + ratio)
    e1 = (i1 - N_EXPERT_GROUPS).astype(F32)
    e2 = (i2 - N_EXPERT_GROUPS).astype(F32)

    rows = logits.shape[0]
    hit1 = col == i1
    hit2 = col == i2
    onehot = jnp.where(hit1, 1.0, jnp.where(hit2, 1.0, 0.0))
    r_i = lax.broadcasted_iota(jnp.int32, (rows, rows), 0)
    c_i = lax.broadcasted_iota(jnp.int32, (rows, rows), 1)
    earlier = jnp.where(c_i < r_i, 1.0, 0.0).astype(BF16)
    before = run_ref[...] + jnp.dot(earlier, onehot.astype(BF16), preferred_element_type=F32)
    rank1 = jnp.sum(jnp.where(hit1, before, 0.0), axis=-1, keepdims=True)
    rank2 = jnp.sum(jnp.where(hit2, before, 0.0), axis=-1, keepdims=True)
    run_ref[...] += jnp.sum(onehot, axis=0, keepdims=True)
    cnt_ref[...] = run_ref[...]

    route_ref[...] = jnp.where(col == 0, e1, jnp.where(col == 1, e2,
                               jnp.where(col == 2, gate1, jnp.where(col == 3, gate2,
                               jnp.where(col == 4, rank1, jnp.where(col == 5, rank2, 0.0))))))


def _mix_norm_route(alpha, y_pool, y_sb, h2d, w_out_b, g, b, w_r2, b_r):
    T, D = h2d.shape
    half = y_pool.shape[1]
    full = lambda a: pl.BlockSpec(a.shape, lambda i: (0,) * a.ndim)
    return pl.pallas_call(
        functools.partial(_mix_kernel, alpha),
        grid=(T // ROW_TILE,),
        in_specs=[pl.BlockSpec((ROW_TILE, half), lambda i: (i, 0)),
                  pl.BlockSpec((ROW_TILE, half), lambda i: (i, 0)),
                  pl.BlockSpec((ROW_TILE, D), lambda i: (i, 0)),
                  full(w_out_b), full(g), full(b), full(w_r2), full(b_r)],
        out_specs=[pl.BlockSpec((ROW_TILE, D), lambda i: (i, 0)),
                   pl.BlockSpec((ROW_TILE, LANES), lambda i: (i, 0)),
                   pl.BlockSpec((1, LANES), lambda i: (0, 0))],
        out_shape=[jax.ShapeDtypeStruct((T, D), F32), jax.ShapeDtypeStruct((T, LANES), F32),
                   jax.ShapeDtypeStruct((1, LANES), F32)],
        scratch_shapes=[pltpu.VMEM((1, LANES), F32)],
        compiler_params=_params(("arbitrary",)),
        name="mix_norm_route",
    )(y_pool, y_sb, h2d, w_out_b, g, b, w_r2, b_r)


def _row_gather_copy(src_hbm, row, dst_ref, dst_row, sem):
    return pltpu.make_async_copy(src_hbm.at[pl.ds(row, 1)], dst_ref.at[pl.ds(dst_row, 1)], sem)


def _invert_kernel(pos_ref, tok_ref):
    def clear(p, carry):
        tok_ref[p] = 0
        return carry

    def place(a, carry):
        tok_ref[pos_ref[a]] = a // TOP_K
        return carry

    lax.fori_loop(0, tok_ref.shape[0], clear, 0)
    lax.fori_loop(0, pos_ref.shape[0], place, 0)


def _invert(pos, n_slots):
    grid_spec = pltpu.PrefetchScalarGridSpec(
        num_scalar_prefetch=1, grid=(1,), in_specs=[],
        out_specs=pl.BlockSpec(memory_space=pltpu.SMEM))
    return pl.pallas_call(
        _invert_kernel, grid_spec=grid_spec,
        out_shape=jax.ShapeDtypeStruct((n_slots,), jnp.int32),
        compiler_params=_params(("arbitrary",)),
        name="invert",
    )(pos)


def _expert_kernel(be_ref, nvb_ref, tok_ref, h_hbm, wg_ref, wu_ref, wd_ref, ys_ref,
                   xbuf, sem, wgb, wub, wdb):
    i = pl.program_id(0)
    nb = pl.num_programs(0)
    rows = DISPATCH_ROWS

    def issue(blk, slot):
        @pl.when(blk < nvb_ref[0])
        def _():
            def body(j, carry):
                for q in range(2):
                    r = 2 * j + q
                    _row_gather_copy(h_hbm, tok_ref[blk * rows + r], xbuf.at[slot], r,
                                     sem.at[slot]).start(priority=q)
                return carry
            lax.fori_loop(0, rows // 2, body, 0, unroll=4)

    @pl.when(i == 0)
    def _():
        issue(0, 0)

    @pl.when(i + 1 < nb)
    def _():
        issue(i + 1, (i + 1) % 2)

    slot = i % 2
    e = be_ref[i]
    prev = be_ref[jnp.maximum(i - 1, 0)]

    @pl.when(jnp.logical_or(i == 0, e != prev))
    def _():
        wgb[...] = wg_ref[...].astype(BF16)
        wub[...] = wu_ref[...].astype(BF16)
        wdb[...] = wd_ref[...].astype(BF16)

    @pl.when(i < nvb_ref[0])
    def _():
        pltpu.make_async_copy(h_hbm.at[pl.ds(0, rows)], xbuf.at[slot], sem.at[slot]).wait()
        x = xbuf[slot].astype(BF16)
        gate = jnp.dot(x, wgb[...], preferred_element_type=F32)
        up = jnp.dot(x, wub[...], preferred_element_type=F32)
        hid = (gate * jax.nn.sigmoid(gate) * up).astype(BF16)
        ys_ref[...] = jnp.dot(hid, wdb[...], preferred_element_type=F32)

    @pl.when(i >= nvb_ref[0])
    def _():
        ys_ref[...] = jnp.zeros_like(ys_ref)


def _experts(layer, block_e, n_valid_blocks, slot_tok, h1, w_gate, w_up, w_down):
    T, D = h1.shape
    DE = w_gate.shape[-1]
    rows = DISPATCH_ROWS
    n_blocks = block_e.shape[0]
    w_map = lambda i, be, nvb, tok: (layer, be[i], 0, 0)
    grid_spec = pltpu.PrefetchScalarGridSpec(
        num_scalar_prefetch=3,
        grid=(n_blocks,),
        in_specs=[pl.BlockSpec(memory_space=pl.ANY),
                  pl.BlockSpec((None, None, D, DE), w_map),
                  pl.BlockSpec((None, None, D, DE), w_map),
                  pl.BlockSpec((None, None, DE, D), w_map)],
        out_specs=pl.BlockSpec((rows, D), lambda i, be, nvb, tok: (i, 0)),
        scratch_shapes=[pltpu.VMEM((2, rows, D), F32), pltpu.SemaphoreType.DMA((2,)),
                        pltpu.VMEM((D, DE), BF16), pltpu.VMEM((D, DE), BF16),
                        pltpu.VMEM((DE, D), BF16)],
    )
    return pl.pallas_call(
        _expert_kernel,
        grid_spec=grid_spec,
        out_shape=jax.ShapeDtypeStruct((n_blocks * rows, D), F32),
        compiler_params=_params(("arbitrary",)),
        name="experts",
    )(block_e, n_valid_blocks, slot_tok, h1, w_gate, w_up, w_down)


def _combine_kernel(alpha, with_proj, pos_ref, ys_hbm, h_ref, route_ref, g_ref, b_ref, *refs):
    if with_proj:
        w_refs, o_ref, out_refs = refs[0:4], refs[4], refs[5:9]
        buf_a, buf_b, sem, fence_sem, xb_ref, xp_ref = refs[9:]
    else:
        o_ref, buf_a, buf_b, sem, fence_sem = refs
    i = pl.program_id(0)
    nb = pl.num_programs(0)
    tile = o_ref.shape[0]
    half = tile // 2
    bufs = (buf_a, buf_b)

    def issue(tile_idx, part, lo, hi):
        base = tile_idx * tile + part * half
        for r in range(lo, hi):
            for k in range(TOP_K):
                _row_gather_copy(ys_hbm, pos_ref[(base + r) * TOP_K + k], bufs[part].at[k], r,
                                 sem.at[part]).start(priority=k)

    def wait(part):
        for k in range(TOP_K):
            pltpu.make_async_copy(ys_hbm.at[pl.ds(0, half)], bufs[part].at[k],
                                  sem.at[part]).wait()

    def finish(part):
        rows = slice(part * half, (part + 1) * half)
        route = route_ref[rows, :]
        moe = route[:, 2:3] * bufs[part][0] + route[:, 3:4] * bufs[part][1]
        o_ref[rows, :] = _layer_norm(alpha * h_ref[rows, :] + moe, g_ref[...], b_ref[...])

    def fence():
        pl.semaphore_signal(fence_sem, 1)
        pl.semaphore_wait(fence_sem, 1)

    @pl.when(i == 0)
    def _():
        issue(0, 0, 0, half)
        issue(0, 1, 0, half)

    nxt = jnp.minimum(i + 1, nb - 1)
    wait(0)
    wait(1)
    finish(0)
    fence()
    issue(nxt, 0, 0, half)
    finish(1)
    fence()
    issue(nxt, 1, 0, half)
    if with_proj:
        stages = _projection_stages(o_ref, w_refs, out_refs, xb_ref, xp_ref)
        stages[0]()
        stages[1]()
        fence()
        for stage in stages[2:]:
            stage()

    @pl.when(i == nb - 1)
    def _():
        wait(0)
        wait(1)


def _combine_norm(alpha, pos, ys, h1, route, g, b, proj_weights=None):
    T, D = h1.shape
    rows = ROW_TILE
    with_proj = proj_weights is not None
    const = lambda a: pl.BlockSpec(a.shape, lambda i, pos: (0,) * a.ndim)
    in_specs = [pl.BlockSpec(memory_space=pl.ANY),
                pl.BlockSpec((rows, D), lambda i, pos: (i, 0)),
                pl.BlockSpec((rows, LANES), lambda i, pos: (i, 0)),
                const(g), const(b)]
    out_specs = [pl.BlockSpec((rows, D), lambda i, pos: (i, 0))]
    out_shape = [jax.ShapeDtypeStruct((T, D), F32)]
    args = [pos, ys, h1, route, g, b]
    if with_proj:
        width = proj_weights[0].shape[1]
        in_specs += [const(w) for w in proj_weights]
        args += list(proj_weights)
        row_major = lambda: pl.BlockSpec((rows, width), lambda i, pos: (i, 0))
        feat_major = lambda: pl.BlockSpec((width, rows), lambda i, pos: (0, i))
        out_specs += [row_major(), feat_major(), row_major(), feat_major()]
        out_shape += [jax.ShapeDtypeStruct((T, width), F32), jax.ShapeDtypeStruct((width, T), BF16),
                      jax.ShapeDtypeStruct((T, width), BF16), jax.ShapeDtypeStruct((width, T), BF16)]
    grid_spec = pltpu.PrefetchScalarGridSpec(
        num_scalar_prefetch=1,
        grid=(T // rows,),
        in_specs=in_specs,
        out_specs=out_specs,
        scratch_shapes=[pltpu.VMEM((TOP_K, rows // 2, D), F32), pltpu.VMEM((TOP_K, rows // 2, D), F32),
                        pltpu.SemaphoreType.DMA((2,)), pltpu.SemaphoreType.REGULAR] +
                       ([pltpu.VMEM((rows, D), BF16), pltpu.VMEM((rows, D), BF16)] if with_proj else []),
    )
    out = pl.pallas_call(
        functools.partial(_combine_kernel, alpha, with_proj),
        grid_spec=grid_spec,
        out_shape=out_shape,
        compiler_params=_params(("arbitrary",)),
        name="combine_norm_proj" if with_proj else "combine_norm",
    )(*args)
    return out if with_proj else out[0]


def _dispatch_plan(route, counts_row, n_blocks):
    rows = DISPATCH_ROWS
    experts = jnp.arange(N_EXPERTS, dtype=jnp.int32)
    counts = counts_row[0, N_EXPERT_GROUPS:N_EXPERT_GROUPS + N_EXPERTS].astype(jnp.int32)
    padded = ((counts + rows - 1) // rows) * rows
    pend = jnp.cumsum(padded)
    pstart = pend - padded
    block_start = jnp.arange(n_blocks, dtype=jnp.int32) * rows
    block_e = jnp.minimum(jnp.sum((block_start[:, None] >= pend[None, :]).astype(jnp.int32), axis=1),
                          N_EXPERTS - 1).astype(jnp.int32)
    n_valid_blocks = (pend[-1:] // rows).astype(jnp.int32)
    e = route[:, :TOP_K].astype(jnp.int32)
    rank = route[:, 4:4 + TOP_K].astype(jnp.int32)
    base = jnp.sum(jnp.where(e[:, :, None] == experts[None, None, :], pstart[None, None, :], 0),
                   axis=-1)
    pos = (base + rank).reshape(-1).astype(jnp.int32)
    return block_e, n_valid_blocks, pos


def kernel(x, w_in, w_pool, pool_scale, w_out, ln1_g, ln1_b, w_router_group, b_router_group,
           w_router_expert, b_router_expert, w_gate, w_up, w_down, ln2_g, ln2_b):
    B, S, D = x.shape
    depth = w_in.shape[0]
    T = B * S
    alpha = (2.0 * depth) ** 0.25
    pool_width = w_pool.shape[1] * w_pool.shape[2]
    sb_width = (w_in.shape[2] - pool_width) // 3
    n_blocks = -(-T * TOP_K // DISPATCH_ROWS) + N_EXPERTS

    col_scale = jnp.concatenate([jnp.ones((pool_width,), F32),
                                 jnp.full((sb_width,), -(SB_HEAD_DIM ** -0.5), F32),
                                 jnp.ones((2 * sb_width,), F32)])
    w_in_s = w_in * col_scale
    w_u = w_in_s[:, :, :pool_width].astype(BF16)
    w_qt = jnp.swapaxes(w_in_s[:, :, pool_width:pool_width + sb_width], 1, 2).astype(BF16)
    w_k = w_in_s[:, :, pool_width + sb_width:pool_width + 2 * sb_width].astype(BF16)
    w_vt = jnp.swapaxes(w_in_s[:, :, pool_width + 2 * sb_width:], 1, 2).astype(BF16)
    w_pool_b = w_pool.astype(BF16)
    w_out_b = w_out.astype(BF16)
    w_r = jnp.concatenate([w_router_group, w_router_expert,
                           jnp.zeros((depth, D, LANES - N_EXPERT_GROUPS - N_EXPERTS), F32)], axis=-1)
    wr_hi = w_r.astype(BF16)
    wr_lo = (w_r - wr_hi.astype(F32)).astype(BF16)
    w_r2 = jnp.concatenate([wr_hi, wr_lo], axis=-1)
    b_r = jnp.concatenate([b_router_group.astype(F32),
                           b_router_expert.astype(F32).reshape(depth, N_EXPERTS),
                           jnp.zeros((depth, LANES - N_EXPERT_GROUPS - N_EXPERTS), F32)], axis=-1)

    h = x.reshape(T, D)
    u, q_t, k_perm, v_t = _project(h, w_u[0], w_qt[0], w_k[0], w_vt[0])
    for l in range(depth):
        y_pool = _pool(u, w_pool_b[l], pool_scale[l][None, :], B, S)
        y_sb = _attention(q_t, k_perm, v_t, B, S)
        h1, route, counts = _mix_norm_route(alpha, y_pool, y_sb, h, w_out_b[l], ln1_g[l][None, :],
                                            ln1_b[l][None, :], w_r2[l], b_r[l][None, :])
        block_e, n_valid_blocks, pos = _dispatch_plan(route, counts, n_blocks)
        slot_tok = _invert(pos, n_blocks * DISPATCH_ROWS)
        ys = _experts(l, block_e, n_valid_blocks, slot_tok, h1, w_gate, w_up, w_down)
        nxt = None if l + 1 == depth else (w_u[l + 1], w_qt[l + 1], w_k[l + 1], w_vt[l + 1])
        out = _combine_norm(alpha, pos, ys, h1, route, ln2_g[l][None, :], ln2_b[l][None, :], nxt)
        if nxt is None:
            h = out
        else:
            h, u, q_t, k_perm, v_t = out
    return h.reshape(B, S, D)
```

```python
import functools

import jax
import jax.numpy as jnp
from jax import lax
from jax.experimental import pallas as pl
from jax.experimental.pallas import tpu as pltpu

F32 = jnp.float32
BF16 = jnp.bfloat16

POOL_WINDOWS = (2, 4, 8, 16)
SB_HEAD_DIM = 64
N_EXPERT_GROUPS = 4
EXPERTS_PER_GROUP = 8
N_EXPERTS = N_EXPERT_GROUPS * EXPERTS_PER_GROUP
TOP_K = 2
LN_EPS = 1e-5

LANES = 128
SUBLANES = 8
ROW_TILE = 512
QUERY_BLOCK = 256
KEY_BLOCK = 128
KEY_SEG = KEY_BLOCK // SUBLANES
PAIRS_PER_STEP = 4
LOOP_PAIRS = 2
DISPATCH_ROWS = 512
DISPATCH_TILE = 256
NEG_BIG = -1e30
VMEM_LIMIT = 48 * 1024 * 1024


def _params(sem):
    return pltpu.CompilerParams(dimension_semantics=sem, vmem_limit_bytes=VMEM_LIMIT)


def _proj_kernel(x_ref, wu_ref, wqt_ref, wk_ref, wvt_ref, u_ref, qt_ref, k_ref, vt_ref,
                 xb_ref, xp_ref):
    for stage in _projection_stages(x_ref, (wu_ref, wqt_ref, wk_ref, wvt_ref),
                                    (u_ref, qt_ref, k_ref, vt_ref), xb_ref, xp_ref):
        stage()


def _projection_stages(x_ref, w_refs, out_refs, xb_ref, xp_ref):
    wu_ref, wqt_ref, wk_ref, wvt_ref = w_refs
    u_ref, qt_ref, k_ref, vt_ref = out_refs
    rows = x_ref.shape[0]
    nt = (((1,), (1,)), ((), ()))

    def pool_input():
        xb_ref[...] = x_ref[...].astype(BF16)
        u_ref[...] = jnp.dot(xb_ref[...], wu_ref[...], preferred_element_type=F32)

    def queries():
        qt_ref[...] = lax.dot_general(wqt_ref[...], xb_ref[...], nt,
                                      preferred_element_type=F32).astype(qt_ref.dtype)

    def keys():
        p_i = lax.broadcasted_iota(jnp.int32, (KEY_BLOCK, KEY_BLOCK), 0)
        t_i = lax.broadcasted_iota(jnp.int32, (KEY_BLOCK, KEY_BLOCK), 1)
        src = (p_i % SUBLANES) * KEY_SEG + p_i // SUBLANES
        perm = jnp.where(t_i == src, 1.0, 0.0).astype(BF16)
        for g in range(rows // KEY_BLOCK):
            blk = slice(g * KEY_BLOCK, (g + 1) * KEY_BLOCK)
            xp_ref[blk, :] = jnp.dot(perm, xb_ref[blk, :],
                                     preferred_element_type=F32).astype(BF16)
        k_ref[...] = jnp.dot(xp_ref[...], wk_ref[...],
                             preferred_element_type=F32).astype(k_ref.dtype)

    def values():
        vt_ref[...] = lax.dot_general(wvt_ref[...], xp_ref[...], nt,
                                      preferred_element_type=F32).astype(vt_ref.dtype)

    return pool_input, queries, keys, values


def _project(h2d, w_u, w_qt, w_k, w_vt):
    T, D = h2d.shape
    width = w_u.shape[1]
    row_major = lambda: pl.BlockSpec((ROW_TILE, width), lambda i: (i, 0))
    feat_major = lambda: pl.BlockSpec((width, ROW_TILE), lambda i: (0, i))
    full = lambda a: pl.BlockSpec(a.shape, lambda i: (0, 0))
    return pl.pallas_call(
        _proj_kernel,
        grid=(T // ROW_TILE,),
        in_specs=[pl.BlockSpec((ROW_TILE, D), lambda i: (i, 0)),
                  full(w_u), full(w_qt), full(w_k), full(w_vt)],
        out_specs=[row_major(), feat_major(), row_major(), feat_major()],
        out_shape=[jax.ShapeDtypeStruct((T, width), F32),
                   jax.ShapeDtypeStruct((width, T), BF16),
                   jax.ShapeDtypeStruct((T, width), BF16),
                   jax.ShapeDtypeStruct((width, T), BF16)],
        scratch_shapes=[pltpu.VMEM((ROW_TILE, D), BF16), pltpu.VMEM((ROW_TILE, D), BF16)],
        compiler_params=_params(("parallel",)),
        name="proj",
    )(h2d, w_u, w_qt, w_k, w_vt)


def _pool_kernel(u_ref, w_ref, s_ref, o_ref):
    S = u_ref.shape[0]
    row = lax.broadcasted_iota(jnp.int32, (S, LANES), 0)
    for g, win in enumerate(POOL_WINDOWS):
        u = u_ref[:, g * LANES:(g + 1) * LANES]
        acc = u
        k = 1
        while k < win:
            shifted = jnp.where(row >= k, pltpu.roll(acc, k, 0), 0.0)
            acc = acc + shifted
            k *= 2
        count = jnp.minimum(row + 1, win).astype(F32)
        pooled = acc / count - u
        mixed = jnp.dot(pooled.astype(BF16), w_ref[g], preferred_element_type=F32)
        o_ref[:, g * LANES:(g + 1) * LANES] = (
            mixed * s_ref[:, g * LANES:(g + 1) * LANES]).astype(o_ref.dtype)


def _pool(u, w_pool_b, pool_scale, B, S):
    T, W = u.shape
    return pl.pallas_call(
        _pool_kernel,
        grid=(B,),
        in_specs=[pl.BlockSpec((S, W), lambda b: (b, 0)),
                  pl.BlockSpec(w_pool_b.shape, lambda b: (0, 0, 0)),
                  pl.BlockSpec((1, W), lambda b: (0, 0))],
        out_specs=pl.BlockSpec((S, W), lambda b: (b, 0)),
        out_shape=jax.ShapeDtypeStruct((T, W), BF16),
        compiler_params=_params(("parallel",)),
        name="pool",
    )(u, w_pool_b, pool_scale)


def _attn_kernel(qt_ref, k_ref, vt_ref, o_ref, acc_ref, car_ref, z0_ref, z1_ref, a0_ref, a1_ref):
    i = pl.program_id(2)
    qb = QUERY_BLOCK
    kb = KEY_BLOCK
    width = 2 * qb
    pairs = range(PAIRS_PER_STEP)
    pair_rows = [slice(p * LANES, (p + 1) * LANES) for p in pairs]
    feat = lax.broadcasted_iota(jnp.int32, (LANES, qb), 0)
    rhs = []
    for p in pairs:
        q2 = qt_ref[pair_rows[p], :]
        zero = jnp.zeros_like(q2)
        rhs.append(jnp.concatenate([jnp.where(feat < SB_HEAD_DIM, q2, zero),
                                    jnp.where(feat >= SB_HEAD_DIM, q2, zero)], axis=1))
    sub = lax.broadcasted_iota(jnp.int32, (SUBLANES, width), 0)
    lane = lax.broadcasted_iota(jnp.int32, (SUBLANES, width), 1)
    q_pos = i * qb + jnp.bitwise_and(lane, qb - 1)

    acc_ref[...] = jnp.zeros_like(acc_ref)
    car_ref[...] = jnp.ones_like(car_ref)

    n_diag = qb // kb
    last = (i + 1) * n_diag - 1

    def block_start(n):
        return pl.multiple_of(jnp.clip(last - n, 0, last) * kb, kb)

    def scores(n, z_ref, ps=pairs):
        for p in ps:
            z_ref[p] = jnp.dot(k_ref[pl.ds(block_start(n), kb), pair_rows[p]], rhs[p],
                               preferred_element_type=F32)

    def values(n, a_ref, ps=pairs):
        for p in ps:
            acc_ref[p] += jnp.dot(vt_ref[pair_rows[p], pl.ds(block_start(n), kb)], a_ref[p],
                                  preferred_element_type=F32)

    def weights(n, z_ref, a_ref, masked, ps=pairs):
        for p in ps:
            weights_one(n, z_ref.at[p], a_ref.at[p], car_ref.at[p], masked)

    def weights_one(n, z_ref, a_ref, car_ref, masked):
        start = block_start(n)
        run = jnp.ones((SUBLANES, width), F32)
        for v in reversed(range(KEY_SEG)):
            slab = slice(v * SUBLANES, (v + 1) * SUBLANES)
            beta = 1.0 / (1.0 + jnp.exp(z_ref[slab, :]))
            rest = 1.0 - beta
            if masked:
                m = (start + sub * KEY_SEG + v) < q_pos
                beta = jnp.where(m, beta, 0.0)
                rest = jnp.where(m, rest, 1.0)
            z_ref[slab, :] = beta * run
            run = run * rest
        incl = run
        for step in (1, 2, 4):
            shifted = pltpu.roll(incl, SUBLANES - step, 0)
            incl = incl * jnp.where(sub < SUBLANES - step, shifted, 1.0)
        later = jnp.where(sub < SUBLANES - 1, pltpu.roll(incl, SUBLANES - 1, 0), 1.0)
        base = later * car_ref[...]
        a_ref[...] = jnp.concatenate(
            [z_ref[v * SUBLANES:(v + 1) * SUBLANES, :] * base for v in range(KEY_SEG)],
            axis=0).astype(BF16)
        car_ref[...] *= jnp.broadcast_to(incl[0:1, :], (SUBLANES, width))

    assert n_diag == 2
    scores(0, z0_ref)
    scores(1, z1_ref)
    weights(0, z0_ref, a0_ref, True)
    scores(2, z0_ref)
    weights(1, z1_ref, a1_ref, True)
    values(0, a0_ref)

    for g in range(0, PAIRS_PER_STEP, LOOP_PAIRS):
        ps = tuple(range(g, g + LOOP_PAIRS))

        def two_blocks(t, carry, ps=ps):
            n = 2 + 2 * t
            values(n - 1, a1_ref, ps)
            weights(n, z0_ref, a0_ref, False, ps)
            scores(n + 1, z1_ref, ps)
            values(n, a0_ref, ps)
            weights(n + 1, z1_ref, a1_ref, False, ps)
            scores(n + 2, z0_ref, ps)
            return carry

        lax.fori_loop(0, i, two_blocks, 0)
    values(last, a1_ref)

    for p in pairs:
        acc = acc_ref[p]
        both = jnp.where(feat < SB_HEAD_DIM, acc[:, :qb], acc[:, qb:])
        o_ref[:, pair_rows[p]] = both.T.astype(o_ref.dtype)


def _attention(q_t, k_perm, v_t, B, S):
    W, T = q_t.shape
    qb = QUERY_BLOCK
    nq = S // qb
    step_w = PAIRS_PER_STEP * LANES
    groups = W // step_w
    per_pair = lambda shape, dtype: pltpu.VMEM((PAIRS_PER_STEP,) + shape, dtype)
    return pl.pallas_call(
        _attn_kernel,
        grid=(B, groups, nq),
        in_specs=[pl.BlockSpec((step_w, qb), lambda b, g, i: (g, b * nq + i)),
                  pl.BlockSpec((S, step_w), lambda b, g, i: (b, g)),
                  pl.BlockSpec((step_w, S), lambda b, g, i: (g, b))],
        out_specs=pl.BlockSpec((qb, step_w), lambda b, g, i: (b * nq + i, g)),
        out_shape=jax.ShapeDtypeStruct((T, W), BF16),
        scratch_shapes=[per_pair((LANES, 2 * qb), F32), per_pair((SUBLANES, 2 * qb), F32),
                        per_pair((KEY_BLOCK, 2 * qb), F32), per_pair((KEY_BLOCK, 2 * qb), F32),
                        per_pair((KEY_BLOCK, 2 * qb), BF16), per_pair((KEY_BLOCK, 2 * qb), BF16)],
        compiler_params=_params(("parallel", "parallel", "parallel")),
        name="attn",
    )(q_t, k_perm, v_t)


def _layer_norm(y, g, b):
    mu = jnp.mean(y, axis=-1, keepdims=True)
    yc = y - mu
    var = jnp.mean(yc * yc, axis=-1, keepdims=True)
    return yc * lax.rsqrt(var + LN_EPS) * g + b


def _mix_kernel(alpha, yp_ref, ys_ref, h_ref, wo_ref, g_ref, b_ref, wr_ref, br_ref,
                h1_ref, route_ref, cnt_ref, run_ref):
    @pl.when(pl.program_id(0) == 0)
    def _():
        run_ref[...] = jnp.zeros_like(run_ref)

    half = yp_ref.shape[1]
    mix = (jnp.dot(yp_ref[...], wo_ref[:half, :], preferred_element_type=F32) +
           jnp.dot(ys_ref[...], wo_ref[half:, :], preferred_element_type=F32))
    h1 = _layer_norm(alpha * h_ref[...] + mix, g_ref[...], b_ref[...])
    h1_ref[...] = h1

    hh = h1.astype(BF16)
    hl = (h1 - hh.astype(F32)).astype(BF16)
    prod_h = jnp.dot(hh, wr_ref[...], preferred_element_type=F32)
    prod_l = jnp.dot(hl, wr_ref[:, :LANES], preferred_element_type=F32)
    logits = prod_h[:, :LANES] + prod_l + prod_h[:, LANES:] + br_ref[...]
    col = lax.broadcasted_iota(jnp.int32, logits.shape, 1)
    gl = jnp.where(col < N_EXPERT_GROUPS, logits, NEG_BIG)
    gmax = jnp.max(gl, axis=-1, keepdims=True)
    gidx = jnp.min(jnp.where(gl == gmax, col, LANES), axis=-1, keepdims=True)
    g_p = 1.0 / jnp.sum(jnp.exp(gl - gmax), axis=-1, keepdims=True)
    lo_col = N_EXPERT_GROUPS + EXPERTS_PER_GROUP * gidx
    el = jnp.where(col >= lo_col, jnp.where(col < lo_col + EXPERTS_PER_GROUP, logits, NEG_BIG),
                   NEG_BIG)
    m1 = jnp.max(el, axis=-1, keepdims=True)
    i1 = jnp.min(jnp.where(el == m1, col, LANES), axis=-1, keepdims=True)
    el2 = jnp.where(col == i1, NEG_BIG, el)
    m2 = jnp.max(el2, axis=-1, keepdims=True)
    i2 = jnp.min(jnp.where(el2 == m2, col, LANES), axis=-1, keepdims=True)
    ratio = jnp.exp(m2 - m1)
    gate1 = g_p / (1.0 + ratio)
    gate2 = g_p * ratio / (1.0 + ratio)
    e1 = (i1 - N_EXPERT_GROUPS).astype(F32)
    e2 = (i2 - N_EXPERT_GROUPS).astype(F32)

    rows = logits.shape[0]
    hit1 = col == i1
    hit2 = col == i2
    onehot = jnp.where(hit1, 1.0, jnp.where(hit2, 1.0, 0.0))
    r_i = lax.broadcasted_iota(jnp.int32, (rows, rows), 0)
    c_i = lax.broadcasted_iota(jnp.int32, (rows, rows), 1)
    earlier = jnp.where(c_i < r_i, 1.0, 0.0).astype(BF16)
    before = run_ref[...] + jnp.dot(earlier, onehot.astype(BF16), preferred_element_type=F32)
    rank1 = jnp.sum(jnp.where(hit1, before, 0.0), axis=-1, keepdims=True)
    rank2 = jnp.sum(jnp.where(hit2, before, 0.0), axis=-1, keepdims=True)
    run_ref[...] += jnp.sum(onehot, axis=0, keepdims=True)
    cnt_ref[...] = run_ref[...]

    route_ref[...] = jnp.where(col == 0, e1, jnp.where(col == 1, e2,
                               jnp.where(col == 2, gate1, jnp.where(col == 3, gate2,
                               jnp.where(col == 4, rank1, jnp.where(col == 5, rank2, 0.0))))))


def _mix_norm_route(alpha, y_pool, y_sb, h2d, w_out_b, g, b, w_r2, b_r):
    T, D = h2d.shape
    half = y_pool.shape[1]
    full = lambda a: pl.BlockSpec(a.shape, lambda i: (0,) * a.ndim)
    return pl.pallas_call(
        functools.partial(_mix_kernel, alpha),
        grid=(T // ROW_TILE,),
        in_specs=[pl.BlockSpec((ROW_TILE, half), lambda i: (i, 0)),
                  pl.BlockSpec((ROW_TILE, half), lambda i: (i, 0)),
                  pl.BlockSpec((ROW_TILE, D), lambda i: (i, 0)),
                  full(w_out_b), full(g), full(b), full(w_r2), full(b_r)],
        out_specs=[pl.BlockSpec((ROW_TILE, D), lambda i: (i, 0)),
                   pl.BlockSpec((ROW_TILE, LANES), lambda i: (i, 0)),
                   pl.BlockSpec((1, LANES), lambda i: (0, 0))],
        out_shape=[jax.ShapeDtypeStruct((T, D), F32), jax.ShapeDtypeStruct((T, LANES), F32),
                   jax.ShapeDtypeStruct((1, LANES), F32)],
        scratch_shapes=[pltpu.VMEM((1, LANES), F32)],
        compiler_params=_params(("arbitrary",)),
        name="mix_norm_route",
    )(y_pool, y_sb, h2d, w_out_b, g, b, w_r2, b_r)


def _row_gather_copy(src_hbm, row, dst_ref, dst_row, sem):
    return pltpu.make_async_copy(src_hbm.at[pl.ds(row, 1)], dst_ref.at[pl.ds(dst_row, 1)], sem)


def _row_scatter_copy(src_ref, src_row, dst_hbm, row, sem):
    return pltpu.make_async_copy(src_ref.at[pl.ds(src_row, 1)], dst_hbm.at[pl.ds(row, 1)], sem)


def _dispatch_kernel(pos_ref, tail_ref, nvb_ref, h_ref, xs_hbm, sbuf, zbuf, sem, zsem):
    i = pl.program_id(0)
    nb = pl.num_programs(0)
    rows = DISPATCH_TILE
    blk = DISPATCH_ROWS
    n_blocks = xs_hbm.shape[0] // blk

    def zero_copy(start):
        return pltpu.make_async_copy(zbuf, xs_hbm.at[pl.ds(pl.multiple_of(start, blk), blk)], zsem)

    @pl.when(i == 0)
    def _():
        zbuf[...] = jnp.zeros_like(zbuf)
        for e in range(N_EXPERTS):
            @pl.when(tail_ref[e] >= 0)
            def _():
                zero_copy(tail_ref[e]).start()

        def start_unused(b, carry):
            zero_copy(b * blk).start()
            return carry

        def wait_unused(b, carry):
            zero_copy(b * blk).wait()
            return carry

        lax.fori_loop(nvb_ref[0], n_blocks, start_unused, 0)
        for e in range(N_EXPERTS):
            @pl.when(tail_ref[e] >= 0)
            def _():
                zero_copy(tail_ref[e]).wait()
        lax.fori_loop(nvb_ref[0], n_blocks, wait_unused, 0)

    slot = i % 2

    def wait_slot(s):
        for _ in range(TOP_K):
            pltpu.make_async_copy(sbuf.at[s], xs_hbm.at[pl.ds(0, rows)], sem.at[s]).wait()

    @pl.when(i >= 2)
    def _():
        wait_slot(slot)

    sbuf[slot] = h_ref[...]

    def body(r, carry):
        for k in range(TOP_K):
            _row_scatter_copy(sbuf.at[slot], r, xs_hbm, pos_ref[(i * rows + r) * TOP_K + k],
                              sem.at[slot]).start(priority=k)
        return carry

    lax.fori_loop(0, rows, body, 0, unroll=8)

    @pl.when(i == nb - 1)
    def _():
        @pl.when(nb >= 2)
        def _():
            wait_slot(1 - slot)
        wait_slot(slot)


def _dispatch(pos, tail, n_valid_blocks, h1, n_blocks):
    T, D = h1.shape
    rows = DISPATCH_TILE
    grid_spec = pltpu.PrefetchScalarGridSpec(
        num_scalar_prefetch=3,
        grid=(T // rows,),
        in_specs=[pl.BlockSpec((rows, D), lambda i, pos, tail, nvb: (i, 0))],
        out_specs=pl.BlockSpec(memory_space=pl.ANY),
        scratch_shapes=[pltpu.VMEM((2, rows, D), F32), pltpu.VMEM((DISPATCH_ROWS, D), F32),
                        pltpu.SemaphoreType.DMA((2,)), pltpu.SemaphoreType.DMA],
    )
    return pl.pallas_call(
        _dispatch_kernel,
        grid_spec=grid_spec,
        out_shape=jax.ShapeDtypeStruct((n_blocks * DISPATCH_ROWS, D), F32),
        compiler_params=_params(("arbitrary",)),
        name="dispatch",
    )(pos, tail, n_valid_blocks, h1)


def _expert_kernel(be_ref, nvb_ref, xs_ref, wg_ref, wu_ref, wd_ref, ys_ref, wgb, wub, wdb):
    i = pl.program_id(0)
    e = be_ref[i]
    prev = be_ref[jnp.maximum(i - 1, 0)]

    @pl.when(jnp.logical_or(i == 0, e != prev))
    def _():
        wgb[...] = wg_ref[...].astype(BF16)
        wub[...] = wu_ref[...].astype(BF16)
        wdb[...] = wd_ref[...].astype(BF16)

    @pl.when(i < nvb_ref[0])
    def _():
        x = xs_ref[...].astype(BF16)
        gate = jnp.dot(x, wgb[...], preferred_element_type=F32)
        up = jnp.dot(x, wub[...], preferred_element_type=F32)
        hid = (gate * jax.nn.sigmoid(gate) * up).astype(BF16)
        ys_ref[...] = jnp.dot(hid, wdb[...], preferred_element_type=F32)

    @pl.when(i >= nvb_ref[0])
    def _():
        ys_ref[...] = jnp.zeros_like(ys_ref)


def _experts(layer, block_e, n_valid_blocks, xs, w_gate, w_up, w_down):
    P, D = xs.shape
    DE = w_gate.shape[-1]
    rows = DISPATCH_ROWS
    n_blocks = P // rows
    x_map = lambda i, be, nvb: (jnp.minimum(i, jnp.maximum(nvb[0] - 1, 0)), 0)
    grid_spec = pltpu.PrefetchScalarGridSpec(
        num_scalar_prefetch=2,
        grid=(n_blocks,),
        in_specs=[pl.BlockSpec((rows, D), x_map),
                  pl.BlockSpec((None, None, D, DE), lambda i, be, nvb: (layer, be[i], 0, 0)),
                  pl.BlockSpec((None, None, D, DE), lambda i, be, nvb: (layer, be[i], 0, 0)),
                  pl.BlockSpec((None, None, DE, D), lambda i, be, nvb: (layer, be[i], 0, 0))],
        out_specs=pl.BlockSpec((rows, D), lambda i, be, nvb: (i, 0)),
        scratch_shapes=[pltpu.VMEM((D, DE), BF16), pltpu.VMEM((D, DE), BF16),
                        pltpu.VMEM((DE, D), BF16)],
    )
    return pl.pallas_call(
        _expert_kernel,
        grid_spec=grid_spec,
        out_shape=jax.ShapeDtypeStruct((P, D), F32),
        compiler_params=_params(("arbitrary",)),
        name="experts",
    )(block_e, n_valid_blocks, xs, w_gate, w_up, w_down)


def _combine_kernel(alpha, with_proj, pos_ref, ys_hbm, h_ref, route_ref, g_ref, b_ref, *refs):
    if with_proj:
        w_refs, o_ref, out_refs = refs[0:4], refs[4], refs[5:9]
        buf_a, buf_b, sem, fence_sem, xb_ref, xp_ref = refs[9:]
    else:
        o_ref, buf_a, buf_b, sem, fence_sem = refs
    i = pl.program_id(0)
    nb = pl.num_programs(0)
    tile = o_ref.shape[0]
    half = tile // 2
    bufs = (buf_a, buf_b)

    def issue(tile_idx, part, lo, hi):
        base = tile_idx * tile + part * half
        for r in range(lo, hi):
            for k in range(TOP_K):
                _row_gather_copy(ys_hbm, pos_ref[(base + r) * TOP_K + k], bufs[part].at[k], r,
                                 sem.at[part]).start(priority=k)

    def wait(part):
        for k in range(TOP_K):
            pltpu.make_async_copy(ys_hbm.at[pl.ds(0, half)], bufs[part].at[k],
                                  sem.at[part]).wait()

    def finish(part):
        rows = slice(part * half, (part + 1) * half)
        route = route_ref[rows, :]
        moe = route[:, 2:3] * bufs[part][0] + route[:, 3:4] * bufs[part][1]
        o_ref[rows, :] = _layer_norm(alpha * h_ref[rows, :] + moe, g_ref[...], b_ref[...])

    def fence():
        pl.semaphore_signal(fence_sem, 1)
        pl.semaphore_wait(fence_sem, 1)

    @pl.when(i == 0)
    def _():
        issue(0, 0, 0, half)
        issue(0, 1, 0, half)

    nxt = jnp.minimum(i + 1, nb - 1)
    wait(0)
    wait(1)
    finish(0)
    fence()
    issue(nxt, 0, 0, half)
    finish(1)
    fence()
    issue(nxt, 1, 0, half)
    if with_proj:
        stages = _projection_stages(o_ref, w_refs, out_refs, xb_ref, xp_ref)
        stages[0]()
        stages[1]()
        fence()
        for stage in stages[2:]:
            stage()

    @pl.when(i == nb - 1)
    def _():
        wait(0)
        wait(1)


def _combine_norm(alpha, pos, ys, h1, route, g, b, proj_weights=None):
    T, D = h1.shape
    rows = ROW_TILE
    with_proj = proj_weights is not None
    const = lambda a: pl.BlockSpec(a.shape, lambda i, pos: (0,) * a.ndim)
    in_specs = [pl.BlockSpec(memory_space=pl.ANY),
                pl.BlockSpec((rows, D), lambda i, pos: (i, 0)),
                pl.BlockSpec((rows, LANES), lambda i, pos: (i, 0)),
                const(g), const(b)]
    out_specs = [pl.BlockSpec((rows, D), lambda i, pos: (i, 0))]
    out_shape = [jax.ShapeDtypeStruct((T, D), F32)]
    args = [pos, ys, h1, route, g, b]
    if with_proj:
        width = proj_weights[0].shape[1]
        in_specs += [const(w) for w in proj_weights]
        args += list(proj_weights)
        row_major = lambda: pl.BlockSpec((rows, width), lambda i, pos: (i, 0))
        feat_major = lambda: pl.BlockSpec((width, rows), lambda i, pos: (0, i))
        out_specs += [row_major(), feat_major(), row_major(), feat_major()]
        out_shape += [jax.ShapeDtypeStruct((T, width), F32), jax.ShapeDtypeStruct((width, T), BF16),
                      jax.ShapeDtypeStruct((T, width), BF16), jax.ShapeDtypeStruct((width, T), BF16)]
    grid_spec = pltpu.PrefetchScalarGridSpec(
        num_scalar_prefetch=1,
        grid=(T // rows,),
        in_specs=in_specs,
        out_specs=out_specs,
        scratch_shapes=[pltpu.VMEM((TOP_K, rows // 2, D), F32), pltpu.VMEM((TOP_K, rows // 2, D), F32),
                        pltpu.SemaphoreType.DMA((2,)), pltpu.SemaphoreType.REGULAR] +
                       ([pltpu.VMEM((rows, D), BF16), pltpu.VMEM((rows, D), BF16)] if with_proj else []),
    )
    out = pl.pallas_call(
        functools.partial(_combine_kernel, alpha, with_proj),
        grid_spec=grid_spec,
        out_shape=out_shape,
        compiler_params=_params(("arbitrary",)),
        name="combine_norm_proj" if with_proj else "combine_norm",
    )(*args)
    return out if with_proj else out[0]


def _dispatch_plan(route, counts_row, n_blocks):
    rows = DISPATCH_ROWS
    experts = jnp.arange(N_EXPERTS, dtype=jnp.int32)
    counts = counts_row[0, N_EXPERT_GROUPS:N_EXPERT_GROUPS + N_EXPERTS].astype(jnp.int32)
    padded = ((counts + rows - 1) // rows) * rows
    pend = jnp.cumsum(padded)
    pstart = pend - padded
    block_start = jnp.arange(n_blocks, dtype=jnp.int32) * rows
    block_e = jnp.minimum(jnp.sum((block_start[:, None] >= pend[None, :]).astype(jnp.int32), axis=1),
                          N_EXPERTS - 1).astype(jnp.int32)
    n_valid_blocks = (pend[-1:] // rows).astype(jnp.int32)
    tail = jnp.where(counts > 0, pend - rows, -1).astype(jnp.int32)
    e = route[:, :TOP_K].astype(jnp.int32)
    rank = route[:, 4:4 + TOP_K].astype(jnp.int32)
    base = jnp.sum(jnp.where(e[:, :, None] == experts[None, None, :], pstart[None, None, :], 0),
                   axis=-1)
    pos = (base + rank).reshape(-1).astype(jnp.int32)
    return block_e, n_valid_blocks, tail, pos


def kernel(x, w_in, w_pool, pool_scale, w_out, ln1_g, ln1_b, w_router_group, b_router_group,
           w_router_expert, b_router_expert, w_gate, w_up, w_down, ln2_g, ln2_b):
    B, S, D = x.shape
    depth = w_in.shape[0]
    T = B * S
    alpha = (2.0 * depth) ** 0.25
    pool_width = w_pool.shape[1] * w_pool.shape[2]
    sb_width = (w_in.shape[2] - pool_width) // 3
    n_blocks = -(-T * TOP_K // DISPATCH_ROWS) + N_EXPERTS

    col_scale = jnp.concatenate([jnp.ones((pool_width,), F32),
                                 jnp.full((sb_width,), -(SB_HEAD_DIM ** -0.5), F32),
                                 jnp.ones((2 * sb_width,), F32)])
    w_in_s = w_in * col_scale
    w_u = w_in_s[:, :, :pool_width].astype(BF16)
    w_qt = jnp.swapaxes(w_in_s[:, :, pool_width:pool_width + sb_width], 1, 2).astype(BF16)
    w_k = w_in_s[:, :, pool_width + sb_width:pool_width + 2 * sb_width].astype(BF16)
    w_vt = jnp.swapaxes(w_in_s[:, :, pool_width + 2 * sb_width:], 1, 2).astype(BF16)
    w_pool_b = w_pool.astype(BF16)
    w_out_b = w_out.astype(BF16)
    w_r = jnp.concatenate([w_router_group, w_router_expert,
                           jnp.zeros((depth, D, LANES - N_EXPERT_GROUPS - N_EXPERTS), F32)], axis=-1)
    wr_hi = w_r.astype(BF16)
    wr_lo = (w_r - wr_hi.astype(F32)).astype(BF16)
    w_r2 = jnp.concatenate([wr_hi, wr_lo], axis=-1)
    b_r = jnp.concatenate([b_router_group.astype(F32),
                           b_router_expert.astype(F32).reshape(depth, N_EXPERTS),
                           jnp.zeros((depth, LANES - N_EXPERT_GROUPS - N_EXPERTS), F32)], axis=-1)

    h = x.reshape(T, D)
    u, q_t, k_perm, v_t = _project(h, w_u[0], w_qt[0], w_k[0], w_vt[0])
    for l in range(depth):
        y_pool = _pool(u, w_pool_b[l], pool_scale[l][None, :], B, S)
        y_sb = _attention(q_t, k_perm, v_t, B, S)
        h1, route, counts = _mix_norm_route(alpha, y_pool, y_sb, h, w_out_b[l], ln1_g[l][None, :],
                                            ln1_b[l][None, :], w_r2[l], b_r[l][None, :])
        block_e, n_valid_blocks, tail, pos = _dispatch_plan(route, counts, n_blocks)
        xs = _dispatch(pos, tail, n_valid_blocks, h1, n_blocks)
        ys = _experts(l, block_e, n_valid_blocks, xs, w_gate, w_up, w_down)
        nxt = None if l + 1 == depth else (w_u[l + 1], w_qt[l + 1], w_k[l + 1], w_vt[l + 1])
        out = _combine_norm(alpha, pos, ys, h1, route, ln2_g[l][None, :], ln2_b[l][None, :], nxt)
        if nxt is None:
            h = out
        else:
            h, u, q_t, k_perm, v_t = out
    return h.reshape(B, S, D)
```
